```python
import math
import jax, jax.numpy as jnp
from jax import lax
import numpy as np

D_MODEL = 2048
BATCH = 4
SEQ = 4096
DEPTH = 2

GRID_W = 64
CTX_LEN = 256
EPS = 1e-6

CHUNK = 128
A_WIDTH = D_MODEL // 2
A_GROUPS = 8
A_GROUP_DIM = A_WIDTH // A_GROUPS

B_HEADS = 8
B_QK_DIM = 64
B_V_DIM = 2 * B_QK_DIM
B_QK_WIDTH = B_HEADS * 2 * B_QK_DIM
B_WIDTH = B_HEADS * B_V_DIM
Q_BLOCK = 128
ROPE_BASE = 10000.0

C_WIDTH = D_MODEL
C_CONV_W = 31
C_PAD = (C_CONV_W - 1) // 2

A_U_END = A_WIDTH
A_V_END = A_U_END + A_WIDTH
A_G_END = A_V_END + A_WIDTH
B_Q_END = A_G_END + B_QK_WIDTH
B_K_END = B_Q_END + B_QK_WIDTH
B_V_END = B_K_END + B_WIDTH
EVEN_IN = B_V_END + B_WIDTH
EVEN_MIX = A_WIDTH + B_WIDTH
ODD_IN = 3 * C_WIDTH

N_EVEN = (DEPTH + 1) // 2
N_ODD = DEPTH // 2

kernel_name = "hybrid_gmlp_diffattn_conformer_prefix_dit"


def rms_norm(x, g):
    xf = x.astype(jnp.float32)
    y = xf * lax.rsqrt(jnp.mean(xf * xf, axis=-1, keepdims=True) + EPS)
    return (y * g.astype(jnp.float32)).astype(x.dtype)


def layer_norm(x, g, b):
    xf = x.astype(jnp.float32)
    mu = jnp.mean(xf, axis=-1, keepdims=True)
    var = jnp.mean(jnp.square(xf - mu), axis=-1, keepdims=True)
    y = (xf - mu) * lax.rsqrt(var + EPS)
    return (y * g.astype(jnp.float32) + b.astype(jnp.float32)).astype(x.dtype)


def adaln(cond, w, b):
    m = jax.nn.silu(cond) @ w + b
    return jnp.split(m, 3, axis=-1)


def modulate(h, shift, scale):
    return h * (1.0 + scale) + shift


def axial_rope_tables(rows, dim):
    axis_dim = dim // 2
    inv = ROPE_BASE ** (-jnp.arange(0, axis_dim, 2, dtype=jnp.float32) / axis_dim)
    row = jnp.repeat(jnp.arange(rows, dtype=jnp.float32), GRID_W)
    col = jnp.tile(jnp.arange(GRID_W, dtype=jnp.float32), rows)
    ar = row[:, None] * inv[None, :]
    ac = col[:, None] * inv[None, :]
    ang = jnp.concatenate([ar, ar, ac, ac], axis=-1)
    return jnp.cos(ang), jnp.sin(ang)


def apply_axial_rope(t, cos, sin):
    x1, x2, x3, x4 = jnp.split(t, 4, axis=-1)
    rot = jnp.concatenate([-x2, x1, -x4, x3], axis=-1)
    return (t.astype(jnp.float32) * cos + rot.astype(jnp.float32) * sin).astype(t.dtype)


def diff_qk(t, g, cos, sin):
    bsz, L, _ = t.shape
    t = rms_norm(t.reshape(bsz, L, B_HEADS, 2, B_QK_DIM), g).transpose(3, 0, 2, 1, 4)
    if cos is not None:
        t = apply_axial_rope(t, cos, sin)
    return t


def heads_v(t):
    bsz, L, _ = t.shape
    return t.reshape(bsz, L, B_HEADS, B_V_DIM).transpose(0, 2, 1, 3)


def diff_attend(q, k, v, lam):
    s = jnp.einsum('mbhqd,mbhkd->mbhqk', q, k).astype(jnp.float32) * (B_QK_DIM ** -0.5)
    p = jax.nn.softmax(s, axis=-1)
    p = p[0] - lam * p[1]
    return jnp.einsum('bhqk,bhkd->bhqd', p.astype(v.dtype), v)


def blocked_diff_attention(q, k_all, v_all, lam):
    _, bsz, nh, L, d = q.shape
    nb = L // Q_BLOCK
    qb = q.reshape(2, bsz, nh, nb, Q_BLOCK, d).transpose(3, 0, 1, 2, 4, 5)
    ob = lax.map(lambda blk: diff_attend(blk, k_all, v_all, lam), qb)
    return ob.transpose(1, 2, 0, 3, 4).reshape(bsz, nh, L, B_V_DIM)


def diff_head_out(o, onorm_g, lambda_init):
    bsz, nh, L, _ = o.shape
    o = rms_norm(o, onorm_g) * (1.0 - lambda_init)
    return o.transpose(0, 2, 1, 3).reshape(bsz, L, B_WIDTH)


def chunk_gmlp(u, v, vnorm_g, w_s, b_s):
    bsz, L, _ = v.shape
    v = rms_norm(v, vnorm_g).reshape(bsz, L // CHUNK, CHUNK, A_GROUPS, A_GROUP_DIM)
    mixed = jnp.einsum('gqp,bnpgc->bnqgc', w_s, v) + b_s.T[None, None, :, :, None]
    return u * mixed.reshape(bsz, L, A_WIDTH)


def even_branch_out(au, av, ag, ob, bg, a_vnorm_g, a_ws, a_bs, onorm_g, lambda_init):
    ya = chunk_gmlp(jax.nn.gelu(au, approximate=False), jax.nn.gelu(av, approximate=False),
                    a_vnorm_g, a_ws, a_bs) * jax.nn.silu(ag)
    yb = diff_head_out(ob, onorm_g, lambda_init) * jax.nn.silu(bg)
    return jnp.concatenate([ya, yb], axis=-1)


def even_layer(x, ctx, c, c_ctx, norm_g, ada_w, ada_b, w_in, a_vnorm_g, a_ws, a_bs,
               qnorm_g, knorm_g, lam_vecs, onorm_g, w_out, lambda_init, cos, sin, update_ctx):
    sh, sc, gt = (t[:, None, :] for t in adaln(c, ada_w, ada_b))
    csh, csc, cgt = adaln(c_ctx, ada_w, ada_b)
    h = modulate(rms_norm(x, norm_g), sh, sc)
    hc = modulate(rms_norm(ctx, norm_g), csh, csc)
    lv = lam_vecs.astype(jnp.float32)
    lam = jnp.exp(jnp.sum(lv[0] * lv[1])) - jnp.exp(jnp.sum(lv[2] * lv[3])) + lambda_init

    if update_ctx:
        pc = jnp.split(hc @ w_in, [A_U_END, A_V_END, A_G_END, B_Q_END, B_K_END, B_V_END], axis=-1)
        kc_t, vc_t = pc[4], pc[5]
    else:
        kc_t, vc_t = jnp.split(hc @ w_in[:, B_Q_END:B_V_END], [B_QK_WIDTH], axis=-1)
    kc = diff_qk(kc_t, knorm_g, None, None)
    vc = heads_v(vc_t)

    au, av, ag, bq, bk, bv, bg = jnp.split(
        h @ w_in, [A_U_END, A_V_END, A_G_END, B_Q_END, B_K_END, B_V_END], axis=-1)
    q = diff_qk(bq, qnorm_g, cos, sin)
    k = diff_qk(bk, knorm_g, cos, sin)
    v = heads_v(bv)
    k_all = jnp.concatenate([k, kc], axis=3)
    v_all = jnp.concatenate([v, vc], axis=2)
    ob = blocked_diff_attention(q, k_all, v_all, lam)
    y = even_branch_out(au, av, ag, ob, bg, a_vnorm_g, a_ws, a_bs, onorm_g, lambda_init)
    x_new = x + gt * (y @ w_out)

    if update_ctx:
        qc = diff_qk(pc[3], qnorm_g, None, None)
        obc = diff_attend(qc, kc, vc, lam)
        yc = even_branch_out(pc[0], pc[1], pc[2], obc, pc[6], a_vnorm_g, a_ws, a_bs,
                             onorm_g, lambda_init)
        ctx = ctx + cgt * (yc @ w_out)
    return x_new, ctx


def conv_module(h, w_in, dw_w, dw_b, ln_g, ln_b):
    a, b, g = jnp.split(h @ w_in, 3, axis=-1)
    y = a * jax.nn.sigmoid(b)
    y = lax.conv_general_dilated(y, dw_w[:, None, :], window_strides=(1,),
                                 padding=[(C_PAD, C_PAD)],
                                 dimension_numbers=('NWC', 'WIO', 'NWC'),
                                 feature_group_count=C_WIDTH) + dw_b
    y = jax.nn.silu(layer_norm(y, ln_g, ln_b))
    return y * jax.nn.silu(g)


def odd_layer(x, ctx, c, c_ctx, norm_g, ada_w, ada_b, w_in, dw_w, dw_b, ln_g, ln_b, w_out,
              update_ctx):
    sh, sc, gt = (t[:, None, :] for t in adaln(c, ada_w, ada_b))
    h = modulate(rms_norm(x, norm_g), sh, sc)
    x_new = x + gt * (conv_module(h, w_in, dw_w, dw_b, ln_g, ln_b) @ w_out)
    if update_ctx:
        csh, csc, cgt = adaln(c_ctx, ada_w, ada_b)
        hc = modulate(rms_norm(ctx, norm_g), csh, csc)
        ctx = ctx + cgt * (conv_module(hc, w_in, dw_w, dw_b, ln_g, ln_b) @ w_out)
    return x_new, ctx


def setup_inputs(seed: int = 0) -> dict:
    key = jax.random.key(seed)
    ks = iter(jax.random.split(key, 40))

    def nrm(shape, scale):
        return jax.random.normal(next(ks), shape, jnp.float32) * scale

    D = D_MODEL
    return {
        "x": nrm((BATCH, SEQ, D), 1.0),
        "c": nrm((BATCH, D), 1.0),
        "ctx": nrm((BATCH, CTX_LEN, D), 1.0),
        "c_ctx": nrm((D,), 1.0),
        "e_norm_g": 1.0 + nrm((N_EVEN, D), 0.02),
        "e_ada_w": nrm((N_EVEN, D, 3 * D), 0.5 * D ** -0.5),
        "e_ada_b": nrm((N_EVEN, 3 * D), 0.01),
        "e_w_in": nrm((N_EVEN, D, EVEN_IN), D ** -0.5),
        "e_a_vnorm_g": 1.0 + nrm((N_EVEN, A_WIDTH), 0.02),
        "e_a_ws": nrm((N_EVEN, A_GROUPS, CHUNK, CHUNK), CHUNK ** -0.5),
        "e_a_bs": 1.0 + nrm((N_EVEN, A_GROUPS, CHUNK), 0.02),
        "e_b_qnorm_g": 1.0 + nrm((N_EVEN, B_QK_DIM), 0.02),
        "e_b_knorm_g": 1.0 + nrm((N_EVEN, B_QK_DIM), 0.02),
        "e_b_lambda": nrm((N_EVEN, 4, B_QK_DIM), 0.1),
        "e_b_onorm_g": 1.0 + nrm((N_EVEN, B_V_DIM), 0.02),
        "e_w_out": nrm((N_EVEN, EVEN_MIX, D), EVEN_MIX ** -0.5),
        "o_norm_g": 1.0 + nrm((N_ODD, D), 0.02),
        "o_ada_w": nrm((N_ODD, D, 3 * D), 0.5 * D ** -0.5),
        "o_ada_b": nrm((N_ODD, 3 * D), 0.01),
        "o_w_in": nrm((N_ODD, D, ODD_IN), D ** -0.5),
        "o_dw_w": nrm((N_ODD, C_CONV_W, C_WIDTH), C_CONV_W ** -0.5),
        "o_dw_b": nrm((N_ODD, C_WIDTH), 0.01),
        "o_ln_g": 1.0 + nrm((N_ODD, C_WIDTH), 0.02),
        "o_ln_b": nrm((N_ODD, C_WIDTH), 0.01),
        "o_w_out": nrm((N_ODD, C_WIDTH, D), C_WIDTH ** -0.5),
    }


def reference(x, c, ctx, c_ctx, e_norm_g, e_ada_w, e_ada_b, e_w_in, e_a_vnorm_g, e_a_ws,
              e_a_bs, e_b_qnorm_g, e_b_knorm_g, e_b_lambda, e_b_onorm_g, e_w_out,
              o_norm_g, o_ada_w, o_ada_b, o_w_in, o_dw_w, o_dw_b, o_ln_g, o_ln_b, o_w_out):
    n_lat = x.shape[1]
    rows = n_lat // GRID_W
    cos, sin = axial_rope_tables(rows, B_QK_DIM)
    for layer in range(DEPTH):
        update_ctx = any(j % 2 == 0 for j in range(layer + 1, DEPTH))
        i = layer // 2
        if layer % 2 == 0:
            lambda_init = 0.8 - 0.6 * math.exp(-0.3 * layer)
            x, ctx = even_layer(x, ctx, c, c_ctx, e_norm_g[i], e_ada_w[i], e_ada_b[i], e_w_in[i],
                                e_a_vnorm_g[i], e_a_ws[i], e_a_bs[i], e_b_qnorm_g[i],
                                e_b_knorm_g[i], e_b_lambda[i], e_b_onorm_g[i], e_w_out[i],
                                lambda_init, cos, sin, update_ctx)
        else:
            x, ctx = odd_layer(x, ctx, c, c_ctx, o_norm_g[i], o_ada_w[i], o_ada_b[i], o_w_in[i],
                               o_dw_w[i], o_dw_b[i], o_ln_g[i], o_ln_b[i], o_w_out[i],
                               update_ctx)
    return x
```

```python
import functools
import math

import jax
import jax.numpy as jnp
from jax import lax
from jax.experimental import pallas as pl
from jax.experimental.pallas import tpu as pltpu

F32 = jnp.float32
BF16 = jnp.bfloat16

D_MODEL = 2048
GRID_W = 64
EPS = 1e-6
CHUNK = 128
A_WIDTH = 1024
A_GROUPS = 8
B_HEADS = 8
B_QK_DIM = 64
HEAD_W = 128
B_WIDTH = 1024
ROPE_BASE = 10000.0
CONV_W = 31
CONV_PAD = 15
SEC_W = 1024
LANES = 128
HALO = 16
VMEM_LIMIT = 56 * 1024 * 1024
LOG2E = 1.4426950408889634


def _cparams(sem):
    return pltpu.CompilerParams(dimension_semantics=sem, vmem_limit_bytes=VMEM_LIMIT)


def _silu(x):
    return x * jax.nn.sigmoid(x)


def _gelu(x):
    return 0.5 * x * (1.0 + lax.erf(x * (2.0 ** -0.5)))


def _adaln_body(c_ref, w_ref, b_ref, o_ref):
    s = _silu(c_ref[...]).astype(BF16)
    o_ref[...] = jnp.dot(s, w_ref[...].astype(BF16), preferred_element_type=F32) + b_ref[...]


def _adaln(cond, w, b):
    tn = 512
    n = w.shape[1]
    return pl.pallas_call(
        _adaln_body,
        grid=(n // tn,),
        in_specs=[pl.BlockSpec((8, D_MODEL), lambda j: (0, 0)),
                  pl.BlockSpec((D_MODEL, tn), lambda j: (0, j)),
                  pl.BlockSpec((1, tn), lambda j: (0, j))],
        out_specs=pl.BlockSpec((8, tn), lambda j: (0, j)),
        out_shape=jax.ShapeDtypeStruct((8, n), F32),
        compiler_params=_cparams(("parallel",)),
        name="adaln",
    )(cond, w, b.reshape(1, n))


def _norm_modulate(x, ng, sh, sc):
    ms = jnp.mean(x * x, axis=-1, keepdims=True)
    return (x * lax.rsqrt(ms + EPS) * ng) * (1.0 + sc) + sh


def _qk_norm_rope(acc, g_ref, bd_ref, cos_ref, sin_ref, out_ref, rope, scale):
    for hb in range(B_HEADS):
        cols = slice(hb * HEAD_W, (hb + 1) * HEAD_W)
        t = acc[:, cols]
        ms = jnp.dot((t * t).astype(BF16), bd_ref[...], preferred_element_type=F32)
        t = t * lax.rsqrt(ms + EPS) * g_ref[...]
        if rope:
            lane = lax.broadcasted_iota(jnp.int32, t.shape, 1)
            r = jnp.where((lane & 31) < 16, pltpu.roll(t, LANES - 16, 1), pltpu.roll(t, 16, 1))
            t = t * cos_ref[...] + r * sin_ref[...]
        if scale != 1.0:
            t = t * scale
        out_ref[:, cols] = t.astype(BF16)


def _even_in_body(x_ref, sh_ref, sc_ref, ng_ref, w_ref, vng_ref, ws_ref, bsb_ref, gq_ref, gk_ref,
                  cos_ref, sin_ref, bd_ref, *rest, full, tm, q_scale):
    if full:
        ya_ref, q_ref, k_ref, v_ref, sbg_ref, h_s, au_s, vn_s = rest
    else:
        k_ref, v_ref, h_s = rest
    j = pl.program_id(1)
    sec = j if full else j + 4

    @pl.when(j == 0)
    def _():
        h_s[...] = _norm_modulate(x_ref[...], ng_ref[...], sh_ref[...], sc_ref[...]).astype(BF16)

    acc = jnp.dot(h_s[...], w_ref[...], preferred_element_type=F32)

    if full:
        @pl.when(sec == 0)
        def _():
            au_s[...] = _gelu(acc).astype(BF16)

        @pl.when(sec == 1)
        def _():
            gv = _gelu(acc)
            ms = jnp.mean(gv * gv, axis=-1, keepdims=True)
            vn_s[...] = (gv * lax.rsqrt(ms + EPS) * vng_ref[...]).astype(BF16)

        @pl.when(sec == 2)
        def _():
            sg = _silu(acc)
            for c in range(tm // CHUNK):
                rows = slice(c * CHUNK, (c + 1) * CHUNK)
                for g in range(A_GROUPS):
                    cols = slice(g * LANES, (g + 1) * LANES)
                    mixed = jnp.dot(ws_ref[g], vn_s[rows, cols], preferred_element_type=F32) + bsb_ref[g]
                    ya_ref[rows, cols] = (au_s[rows, cols].astype(F32) * mixed * sg[rows, cols]).astype(BF16)

        @pl.when(sec == 3)
        def _():
            _qk_norm_rope(acc, gq_ref, bd_ref, cos_ref, sin_ref, q_ref, True, q_scale)

        @pl.when(sec == 6)
        def _():
            sbg_ref[...] = _silu(acc).astype(BF16)

    @pl.when(sec == 4)
    def _():
        _qk_norm_rope(acc, gk_ref, bd_ref, cos_ref, sin_ref, k_ref, full, 1.0)

    @pl.when(sec == 5)
    def _():
        v_ref[...] = acc.astype(BF16)


def _even_in(x2d, sh, sc, ng, w_bf, vng, ws_bf, bsb, gq, gk, cos, sin, bd, *, full, tm, rows_per_batch,
             q_scale):
    m = x2d.shape[0]
    tiles_per_batch = rows_per_batch // tm
    nsec = 7 if full else 2
    sec0 = 0 if full else 4
    row = lambda i, j: (i, 0)
    const2 = lambda i, j: (0, 0)
    const3 = lambda i, j: (0, 0, 0)
    batch3 = lambda i, j: (i // tiles_per_batch, 0, 0)
    pos = lambda i, j: (i % tiles_per_batch, 0)
    in_specs = [
        pl.BlockSpec((tm, D_MODEL), row),
        pl.BlockSpec((None, 1, D_MODEL), batch3),
        pl.BlockSpec((None, 1, D_MODEL), batch3),
        pl.BlockSpec((1, D_MODEL), const2),
        pl.BlockSpec((D_MODEL, SEC_W), lambda i, j: (0, j + sec0)),
        pl.BlockSpec((1, A_WIDTH), const2),
        pl.BlockSpec((A_GROUPS, CHUNK, CHUNK), const3),
        pl.BlockSpec((A_GROUPS, CHUNK, LANES), const3),
        pl.BlockSpec((1, HEAD_W), const2),
        pl.BlockSpec((1, HEAD_W), const2),
        pl.BlockSpec((tm, HEAD_W), pos),
        pl.BlockSpec((tm, HEAD_W), pos),
        pl.BlockSpec((HEAD_W, HEAD_W), const2),
    ]
    sec_out = jax.ShapeDtypeStruct((m, SEC_W), BF16)
    n_out = 5 if full else 2
    scratch = [pltpu.VMEM((tm, D_MODEL), BF16)]
    if full:
        scratch += [pltpu.VMEM((tm, A_WIDTH), BF16), pltpu.VMEM((tm, A_WIDTH), BF16)]
    return pl.pallas_call(
        functools.partial(_even_in_body, full=full, tm=tm, q_scale=q_scale),
        grid=(m // tm, nsec),
        in_specs=in_specs,
        out_specs=[pl.BlockSpec((tm, SEC_W), row)] * n_out,
        out_shape=[sec_out] * n_out,
        scratch_shapes=scratch,
        compiler_params=_cparams(("parallel", "arbitrary")),
        name="even_in" if full else "even_in_ctx",
    )(x2d, sh, sc, ng, w_bf, vng, ws_bf, bsb, gq, gk, cos, sin, bd)


def _attn_body(q_ref, k_ref, kc_ref, v_ref, vc_ref, sbg_ref, lam_ref, og_ref, o_ref,
               qq_s, v1_s, m_s, acc_s, *, tq, tk, n_lat, n_ctx, lambda_init):
    i = pl.program_id(2)

    @pl.when(i == 0)
    def _():
        v1_s[0:n_lat, 0:HEAD_W] = v_ref[...]
        v1_s[n_lat:n_lat + n_ctx, 0:HEAD_W] = vc_ref[...]
        v1_s[:, HEAD_W:2 * HEAD_W] = jnp.ones((n_lat + n_ctx, HEAD_W), BF16)

    q = q_ref[...]
    lane = lax.broadcasted_iota(jnp.int32, q.shape, 1)
    zero = jnp.zeros_like(q)
    qq_s[0:tq, :] = jnp.where(lane < B_QK_DIM, q, zero)
    qq_s[tq:2 * tq, :] = jnp.where(lane >= B_QK_DIM, q, zero)
    m_s[...] = jnp.full(m_s.shape, -jnp.inf, F32)
    acc_s[...] = jnp.zeros(acc_s.shape, F32)

    def step(kc, v1c):
        s = lax.dot_general(qq_s[...], kc, (((1,), (1,)), ((), ())), preferred_element_type=F32)
        m_old = m_s[...]
        m_new = jnp.maximum(m_old, jnp.max(s, axis=-1, keepdims=True))
        alpha = jnp.exp2(m_old - m_new)
        p = jnp.exp2(s - m_new).astype(BF16)
        acc_s[...] = alpha * acc_s[...] + jnp.dot(p, v1c, preferred_element_type=F32)
        m_s[...] = m_new

    def lat_step(c, carry):
        off = pl.multiple_of(c * tk, tk)
        step(k_ref[pl.ds(off, tk), :], v1_s[pl.ds(off, tk), :])
        return carry

    lax.fori_loop(0, n_lat // tk, lat_step, 0)
    step(kc_ref[...], v1_s[n_lat:n_lat + n_ctx, :])

    lv = lam_ref[...]
    lam = (jnp.exp(jnp.sum(lv[0:1, :] * lv[1:2, :], axis=-1, keepdims=True))
           - jnp.exp(jnp.sum(lv[2:3, :] * lv[3:4, :], axis=-1, keepdims=True)) + lambda_init)
    acc = acc_s[...]
    o1 = acc[0:tq, 0:HEAD_W] / acc[0:tq, HEAD_W:HEAD_W + 1]
    o2 = acc[tq:2 * tq, 0:HEAD_W] / acc[tq:2 * tq, HEAD_W:HEAD_W + 1]
    o = o1 - lam * o2
    ms = jnp.mean(o * o, axis=-1, keepdims=True)
    y = (o * lax.rsqrt(ms + EPS) * og_ref[...]) * (1.0 - lambda_init)
    o_ref[...] = (y * sbg_ref[...].astype(F32)).astype(BF16)


def _attention(q, k, kc, v, vc, sbg, lam_vecs, og, *, tq, tk, lambda_init):
    bsz, n_lat, _ = q.shape
    n_ctx = kc.shape[1]
    qtile = pl.BlockSpec((None, tq, HEAD_W), lambda b, h, i: (b, i, h))
    whole = lambda n: pl.BlockSpec((None, n, HEAD_W), lambda b, h, i: (b, 0, h))
    const2 = lambda b, h, i: (0, 0)
    return pl.pallas_call(
        functools.partial(_attn_body, tq=tq, tk=tk, n_lat=n_lat, n_ctx=n_ctx, lambda_init=lambda_init),
        grid=(bsz, B_HEADS, n_lat // tq),
        in_specs=[qtile, whole(n_lat), whole(n_ctx), whole(n_lat), whole(n_ctx), qtile,
                  pl.BlockSpec((4, B_QK_DIM), const2), pl.BlockSpec((1, HEAD_W), const2)],
        out_specs=qtile,
        out_shape=jax.ShapeDtypeStruct((bsz, n_lat, B_WIDTH), BF16),
        scratch_shapes=[pltpu.VMEM((2 * tq, HEAD_W), BF16),
                        pltpu.VMEM((n_lat + n_ctx, 2 * HEAD_W), BF16),
                        pltpu.VMEM((2 * tq, 1), F32),
                        pltpu.VMEM((2 * tq, 2 * HEAD_W), F32)],
        compiler_params=_cparams(("parallel", "parallel", "arbitrary")),
        name="diff_attn",
    )(q, k, kc, v, vc, sbg, lam_vecs, og)


def _even_out_body(ya_ref, yb_ref, wa_ref, wb_ref, x_ref, gt_ref, o_ref):
    y = (jnp.dot(ya_ref[...], wa_ref[...], preferred_element_type=F32)
         + jnp.dot(yb_ref[...], wb_ref[...], preferred_element_type=F32))
    o_ref[...] = x_ref[...] + gt_ref[...] * y


def _even_out(ya, yb, w_bf, x2d, gt, *, tm, rows_per_batch):
    m = x2d.shape[0]
    tiles_per_batch = rows_per_batch // tm
    row = lambda i: (i, 0)
    return pl.pallas_call(
        _even_out_body,
        grid=(m // tm,),
        in_specs=[pl.BlockSpec((tm, A_WIDTH), row), pl.BlockSpec((tm, B_WIDTH), row),
                  pl.BlockSpec((A_WIDTH, D_MODEL), lambda i: (0, 0)),
                  pl.BlockSpec((B_WIDTH, D_MODEL), lambda i: (1, 0)),
                  pl.BlockSpec((tm, D_MODEL), row),
                  pl.BlockSpec((None, 1, D_MODEL), lambda i: (i // tiles_per_batch, 0, 0))],
        out_specs=pl.BlockSpec((tm, D_MODEL), row),
        out_shape=jax.ShapeDtypeStruct((m, D_MODEL), F32),
        compiler_params=_cparams(("parallel",)),
        name="even_out",
    )(ya, yb, w_bf, w_bf, x2d, gt)


def _odd_in_body(x_ref, sh_ref, sc_ref, ng_ref, w_ref, y_ref, sg_ref, h_s, a_s):
    j = pl.program_id(1)

    @pl.when(j == 0)
    def _():
        h_s[...] = _norm_modulate(x_ref[...], ng_ref[...], sh_ref[...], sc_ref[...]).astype(BF16)

    acc = jnp.dot(h_s[...], w_ref[...], preferred_element_type=F32)

    @pl.when((j == 0) | (j == 2))
    def _():
        a_s[...] = acc

    @pl.when((j == 1) | (j == 3))
    def _():
        y_ref[...] = (a_s[...] * jax.nn.sigmoid(acc)).astype(BF16)

    @pl.when(j >= 4)
    def _():
        sg_ref[...] = _silu(acc).astype(BF16)


def _odd_in(x2d, sh, sc, ng, w_bf, *, tm, rows_per_batch):
    m = x2d.shape[0]
    tiles_per_batch = rows_per_batch // tm
    row = lambda i, j: (i, 0)
    batch3 = lambda i, j: (i // tiles_per_batch, 0, 0)
    wcol = lambda i, j: (0, jnp.where(j < 4, (j % 2) * 2 + j // 2, j))
    out = jax.ShapeDtypeStruct((m, D_MODEL), BF16)
    return pl.pallas_call(
        _odd_in_body,
        grid=(m // tm, 6),
        in_specs=[pl.BlockSpec((tm, D_MODEL), row),
                  pl.BlockSpec((None, 1, D_MODEL), batch3),
                  pl.BlockSpec((None, 1, D_MODEL), batch3),
                  pl.BlockSpec((1, D_MODEL), lambda i, j: (0, 0)),
                  pl.BlockSpec((D_MODEL, SEC_W), wcol)],
        out_specs=[pl.BlockSpec((tm, SEC_W), lambda i, j: (i, jnp.minimum(j // 2, 1))),
                   pl.BlockSpec((tm, SEC_W), lambda i, j: (i, jnp.maximum(j - 4, 0)))],
        out_shape=[out, out],
        scratch_shapes=[pltpu.VMEM((tm, D_MODEL), BF16), pltpu.VMEM((tm, SEC_W), F32)],
        compiler_params=_cparams(("parallel", "arbitrary")),
        name="odd_in",
    )(x2d, sh, sc, ng, w_bf)


def _odd_out_body(y_ref, yp_ref, yn_ref, sg_ref, x_ref, gt_ref, dw_ref, db_ref, lg_ref, lb_ref, w_ref,
                  o_ref, ybuf_s, conv_s, *, tm, tiles_per_batch, rb):
    i = pl.program_id(0)
    t = i % tiles_per_batch
    ybuf_s[0:HALO, :] = jnp.where(t > 0, yp_ref[...].astype(F32), 0.0)
    ybuf_s[HALO:HALO + tm, :] = y_ref[...].astype(F32)
    ybuf_s[HALO + tm:2 * HALO + tm, :] = jnp.where(t < tiles_per_batch - 1, yn_ref[...].astype(F32), 0.0)

    def col_block(cb, carry):
        c0 = pl.multiple_of(cb * LANES, LANES)
        bias = db_ref[:, pl.ds(c0, LANES)]
        for r in range(tm // rb):
            acc = jnp.broadcast_to(bias, (rb, LANES))
            for k in range(CONV_W):
                r0 = r * rb + k + HALO - CONV_PAD
                acc = acc + ybuf_s[r0:r0 + rb, pl.ds(c0, LANES)] * dw_ref[k:k + 1, pl.ds(c0, LANES)]
            conv_s[r * rb:(r + 1) * rb, pl.ds(c0, LANES)] = acc
        return carry

    lax.fori_loop(0, D_MODEL // LANES, col_block, 0)

    c = conv_s[...]
    mu = jnp.mean(c, axis=-1, keepdims=True)
    cc = c - mu
    var = jnp.mean(cc * cc, axis=-1, keepdims=True)
    z = cc * lax.rsqrt(var + EPS) * lg_ref[...] + lb_ref[...]
    u = (_silu(z) * sg_ref[...].astype(F32)).astype(BF16)
    o_ref[...] = x_ref[...] + gt_ref[...] * jnp.dot(u, w_ref[...], preferred_element_type=F32)


def _odd_out(y, sg, x2d, gt, dw_w, dw_b, ln_g, ln_b, w_bf, *, tm, rows_per_batch):
    m = x2d.shape[0]
    tiles_per_batch = rows_per_batch // tm
    hpt = tm // HALO
    last_halo = m // HALO - 1
    row = lambda i: (i, 0)
    const2 = lambda i: (0, 0)
    return pl.pallas_call(
        functools.partial(_odd_out_body, tm=tm, tiles_per_batch=tiles_per_batch, rb=64),
        grid=(m // tm,),
        in_specs=[pl.BlockSpec((tm, D_MODEL), row),
                  pl.BlockSpec((HALO, D_MODEL), lambda i: (jnp.maximum(i * hpt - 1, 0), 0)),
                  pl.BlockSpec((HALO, D_MODEL), lambda i: (jnp.minimum((i + 1) * hpt, last_halo), 0)),
                  pl.BlockSpec((tm, D_MODEL), row),
                  pl.BlockSpec((tm, D_MODEL), row),
                  pl.BlockSpec((None, 1, D_MODEL), lambda i: (i // tiles_per_batch, 0, 0)),
                  pl.BlockSpec((CONV_W + 1, D_MODEL), const2),
                  pl.BlockSpec((1, D_MODEL), const2),
                  pl.BlockSpec((1, D_MODEL), const2),
                  pl.BlockSpec((1, D_MODEL), const2),
                  pl.BlockSpec((D_MODEL, D_MODEL), const2)],
        out_specs=pl.BlockSpec((tm, D_MODEL), row),
        out_shape=jax.ShapeDtypeStruct((m, D_MODEL), F32),
        scratch_shapes=[pltpu.VMEM((tm + 2 * HALO, D_MODEL), F32), pltpu.VMEM((tm, D_MODEL), F32)],
        compiler_params=_cparams(("parallel",)),
        name="odd_out",
    )(y, y, y, sg, x2d, gt, dw_w, dw_b, ln_g, ln_b, w_bf)


def _rope_tables(n_lat):
    axis_dim = B_QK_DIM // 2
    inv = ROPE_BASE ** (-jnp.arange(0, axis_dim, 2, dtype=F32) / axis_dim)
    rows = n_lat // GRID_W
    r = jnp.repeat(jnp.arange(rows, dtype=F32), GRID_W)
    c = jnp.tile(jnp.arange(GRID_W, dtype=F32), rows)
    ar = r[:, None] * inv[None, :]
    ac = c[:, None] * inv[None, :]
    ang = jnp.concatenate([ar, ar, ac, ac], axis=-1)
    sign = jnp.where((jnp.arange(B_QK_DIM) % 32) < 16, -1.0, 1.0).astype(F32)
    cos = jnp.tile(jnp.cos(ang), (1, 2))
    sin = jnp.tile(jnp.sin(ang) * sign[None, :], (1, 2))
    return cos, sin


def kernel(x, c, ctx, c_ctx, e_norm_g, e_ada_w, e_ada_b, e_w_in, e_a_vnorm_g, e_a_ws, e_a_bs, e_b_qnorm_g, e_b_knorm_g, e_b_lambda, e_b_onorm_g, e_w_out, o_norm_g, o_ada_w, o_ada_b, o_w_in, o_dw_w, o_dw_b, o_ln_g, o_ln_b, o_w_out):
    bsz, n_lat, d = x.shape
    n_ctx = ctx.shape[1]
    x2d = x.reshape(bsz * n_lat, d)
    ctx2d = ctx.reshape(bsz * n_ctx, d)
    cond = jnp.concatenate([c, c_ctx[None, :], jnp.zeros((8 - bsz - 1, d), F32)], axis=0)
    row3 = lambda t: t.reshape(t.shape[0], 1, d)

    lambda_init = 0.8 - 0.6 * math.exp(-0.3 * 0)
    mod = _adaln(cond, e_ada_w[0], e_ada_b[0])
    sh, sc, gt = (row3(mod[:bsz, n * d:(n + 1) * d]) for n in range(3))
    csh, csc = (jnp.broadcast_to(mod[bsz:bsz + 1, n * d:(n + 1) * d], (bsz, d)).reshape(bsz, 1, d)
                for n in range(2))
    cos, sin = _rope_tables(n_lat)
    gid = jnp.arange(HEAD_W) // B_QK_DIM
    bd = jnp.where(gid[:, None] == gid[None, :], 1.0 / B_QK_DIM, 0.0).astype(BF16)
    tile2 = lambda g: jnp.tile(g, 2).reshape(1, HEAD_W)
    w_in_bf = e_w_in[0].astype(BF16)
    common = (e_norm_g[0].reshape(1, d), w_in_bf, e_a_vnorm_g[0].reshape(1, A_WIDTH),
              e_a_ws[0].astype(BF16),
              jnp.broadcast_to(e_a_bs[0][:, :, None], (A_GROUPS, CHUNK, LANES)),
              tile2(e_b_qnorm_g[0]), tile2(e_b_knorm_g[0]), cos, sin, bd)
    q_scale = (B_QK_DIM ** -0.5) * LOG2E
    ya, q, k, v, sbg = _even_in(x2d, sh, sc, *common, full=True, tm=512, rows_per_batch=n_lat,
                                q_scale=q_scale)
    kc, vc = _even_in(ctx2d, csh, csc, *common, full=False, tm=n_ctx, rows_per_batch=n_ctx,
                      q_scale=q_scale)
    seq = lambda t, n: t.reshape(bsz, n, B_WIDTH)
    yb = _attention(seq(q, n_lat), seq(k, n_lat), seq(kc, n_ctx), seq(v, n_lat), seq(vc, n_ctx),
                    seq(sbg, n_lat), e_b_lambda[0], e_b_onorm_g[0].reshape(1, HEAD_W),
                    tq=256, tk=512, lambda_init=lambda_init)
    x1 = _even_out(ya, yb.reshape(bsz * n_lat, B_WIDTH), e_w_out[0].astype(BF16), x2d, gt,
                   tm=512, rows_per_batch=n_lat)

    mod = _adaln(cond, o_ada_w[0], o_ada_b[0])
    sh, sc, gt = (row3(mod[:bsz, n * d:(n + 1) * d]) for n in range(3))
    y, sg = _odd_in(x1, sh, sc, o_norm_g[0].reshape(1, d), o_w_in[0].astype(BF16),
                    tm=512, rows_per_batch=n_lat)
    dw = jnp.concatenate([o_dw_w[0], jnp.zeros((1, d), F32)], axis=0)
    x2 = _odd_out(y, sg, x1, gt, dw, o_dw_b[0].reshape(1, d), o_ln_g[0].reshape(1, d),
                  o_ln_b[0].reshape(1, d), o_w_out[0].astype(BF16), tm=256, rows_per_batch=n_lat)
    return x2.reshape(bsz, n_lat, d)
```

```python
import functools
import math

import jax
import jax.numpy as jnp
from jax import lax
from jax.experimental import pallas as pl
from jax.experimental.pallas import tpu as pltpu

F32 = jnp.float32
BF16 = jnp.bfloat16

D_MODEL = 2048
GRID_W = 64
EPS = 1e-6
CHUNK = 128
A_WIDTH = 1024
A_GROUPS = 8
B_HEADS = 8
B_QK_DIM = 64
HEAD_W = 128
B_WIDTH = 1024
ROPE_BASE = 10000.0
CONV_W = 31
CONV_PAD = 15
SEC_W = 1024
LANES = 128
HALO = 16
VMEM_LIMIT = 56 * 1024 * 1024
LOG2E = 1.4426950408889634


def _cparams(sem):
    return pltpu.CompilerParams(dimension_semantics=sem, vmem_limit_bytes=VMEM_LIMIT)


def _silu(x):
    return x * jax.nn.sigmoid(x)


def _gelu(x):
    return 0.5 * x * (1.0 + lax.erf(x * (2.0 ** -0.5)))


def _adaln_body(c_ref, w_ref, b_ref, o_ref):
    s = _silu(c_ref[...]).astype(BF16)
    o_ref[...] = jnp.dot(s, w_ref[...].astype(BF16), preferred_element_type=F32) + b_ref[...]


def _adaln(cond, w, b):
    tn = 512
    n = w.shape[1]
    return pl.pallas_call(
        _adaln_body,
        grid=(n // tn,),
        in_specs=[pl.BlockSpec((8, D_MODEL), lambda j: (0, 0)),
                  pl.BlockSpec((D_MODEL, tn), lambda j: (0, j)),
                  pl.BlockSpec((1, tn), lambda j: (0, j))],
        out_specs=pl.BlockSpec((8, tn), lambda j: (0, j)),
        out_shape=jax.ShapeDtypeStruct((8, n), F32),
        compiler_params=_cparams(("parallel",)),
        name="adaln",
    )(cond, w, b.reshape(1, n))


def _norm_modulate(x, ng, sh, sc):
    ms = jnp.mean(x * x, axis=-1, keepdims=True)
    return (x * lax.rsqrt(ms + EPS) * ng) * (1.0 + sc) + sh


def _qk_norm_rope(acc, g_ref, bd_ref, cos_ref, sin_ref, out_ref, rope, scale):
    for hb in range(B_HEADS):
        cols = slice(hb * HEAD_W, (hb + 1) * HEAD_W)
        t = acc[:, cols]
        ms = jnp.dot((t * t).astype(BF16), bd_ref[...], preferred_element_type=F32)
        t = t * lax.rsqrt(ms + EPS) * g_ref[...]
        if rope:
            lane = lax.broadcasted_iota(jnp.int32, t.shape, 1)
            r = jnp.where((lane & 31) < 16, pltpu.roll(t, LANES - 16, 1), pltpu.roll(t, 16, 1))
            t = t * cos_ref[...] + r * sin_ref[...]
        if scale != 1.0:
            t = t * scale
        out_ref[:, cols] = t.astype(BF16)


def _even_in_body(x_ref, sh_ref, sc_ref, ng_ref, w_ref, vng_ref, ws_ref, bsb_ref, gq_ref, gk_ref,
                  cos_ref, sin_ref, bd_ref, *rest, full, tm, q_scale):
    if full:
        ya_ref, q_ref, k_ref, v_ref, sbg_ref, h_s, au_s, vn_s = rest
    else:
        k_ref, v_ref, h_s = rest
    j = pl.program_id(1)
    sec = j if full else j + 4

    @pl.when(j == 0)
    def _():
        h_s[...] = _norm_modulate(x_ref[...], ng_ref[...], sh_ref[...], sc_ref[...]).astype(BF16)

    acc = jnp.dot(h_s[...], w_ref[...], preferred_element_type=F32)

    if full:
        @pl.when(sec == 0)
        def _():
            au_s[...] = _gelu(acc).astype(BF16)

        @pl.when(sec == 1)
        def _():
            gv = _gelu(acc)
            ms = jnp.mean(gv * gv, axis=-1, keepdims=True)
            vn_s[...] = (gv * lax.rsqrt(ms + EPS) * vng_ref[...]).astype(BF16)

        @pl.when(sec == 2)
        def _():
            sg = _silu(acc)
            for c in range(tm // CHUNK):
                rows = slice(c * CHUNK, (c + 1) * CHUNK)
                for g in range(A_GROUPS):
                    cols = slice(g * LANES, (g + 1) * LANES)
                    mixed = jnp.dot(ws_ref[g], vn_s[rows, cols], preferred_element_type=F32) + bsb_ref[g]
                    ya_ref[rows, cols] = (au_s[rows, cols].astype(F32) * mixed * sg[rows, cols]).astype(BF16)

        @pl.when(sec == 3)
        def _():
            _qk_norm_rope(acc, gq_ref, bd_ref, cos_ref, sin_ref, q_ref, True, q_scale)

        @pl.when(sec == 6)
        def _():
            sbg_ref[...] = _silu(acc).astype(BF16)

    @pl.when(sec == 4)
    def _():
        _qk_norm_rope(acc, gk_ref, bd_ref, cos_ref, sin_ref, k_ref, full, 1.0)

    @pl.when(sec == 5)
    def _():
        v_ref[...] = acc.astype(BF16)


def _even_in(x2d, sh, sc, ng, w_bf, vng, ws_bf, bsb, gq, gk, cos, sin, bd, *, full, tm, rows_per_batch,
             q_scale):
    m = x2d.shape[0]
    tiles_per_batch = rows_per_batch // tm
    nsec = 7 if full else 2
    sec0 = 0 if full else 4
    row = lambda i, j: (i, 0)
    const2 = lambda i, j: (0, 0)
    const3 = lambda i, j: (0, 0, 0)
    batch3 = lambda i, j: (i // tiles_per_batch, 0, 0)
    pos = lambda i, j: (i % tiles_per_batch, 0)
    in_specs = [
        pl.BlockSpec((tm, D_MODEL), row),
        pl.BlockSpec((None, 1, D_MODEL), batch3),
        pl.BlockSpec((None, 1, D_MODEL), batch3),
        pl.BlockSpec((1, D_MODEL), const2),
        pl.BlockSpec((D_MODEL, SEC_W), lambda i, j: (0, j + sec0)),
        pl.BlockSpec((1, A_WIDTH), const2),
        pl.BlockSpec((A_GROUPS, CHUNK, CHUNK), const3),
        pl.BlockSpec((A_GROUPS, CHUNK, LANES), const3),
        pl.BlockSpec((1, HEAD_W), const2),
        pl.BlockSpec((1, HEAD_W), const2),
        pl.BlockSpec((tm, HEAD_W), pos),
        pl.BlockSpec((tm, HEAD_W), pos),
        pl.BlockSpec((HEAD_W, HEAD_W), const2),
    ]
    sec_out = jax.ShapeDtypeStruct((m, SEC_W), BF16)
    n_out = 5 if full else 2
    scratch = [pltpu.VMEM((tm, D_MODEL), BF16)]
    if full:
        scratch += [pltpu.VMEM((tm, A_WIDTH), BF16), pltpu.VMEM((tm, A_WIDTH), BF16)]
    return pl.pallas_call(
        functools.partial(_even_in_body, full=full, tm=tm, q_scale=q_scale),
        grid=(m // tm, nsec),
        in_specs=in_specs,
        out_specs=[pl.BlockSpec((tm, SEC_W), row)] * n_out,
        out_shape=[sec_out] * n_out,
        scratch_shapes=scratch,
        compiler_params=_cparams(("parallel", "arbitrary")),
        name="even_in" if full else "even_in_ctx",
    )(x2d, sh, sc, ng, w_bf, vng, ws_bf, bsb, gq, gk, cos, sin, bd)


def _attn_body(q_ref, k_ref, kc_ref, v_ref, vc_ref, sbg_ref, lam_ref, og_ref, o_ref,
               kk_s, v1_s, s1_s, s2_s, *, tq, tk, n_lat, n_ctx, lambda_init):
    i = pl.program_id(2)
    n_tot = n_lat + n_ctx
    chunks = [(c0, min(tk, n_tot - c0)) for c0 in range(0, n_tot, tk)]

    @pl.when(i == 0)
    def _():
        kk_s[0:n_lat, :] = k_ref[...]
        kk_s[n_lat:n_lat + n_ctx, :] = kc_ref[...]
        v1_s[0:n_lat, 0:HEAD_W] = v_ref[...]
        v1_s[n_lat:n_lat + n_ctx, 0:HEAD_W] = vc_ref[...]
        v1_s[:, HEAD_W:2 * HEAD_W] = jnp.ones((n_lat + n_ctx, HEAD_W), BF16)

    q = q_ref[...]
    lane = lax.broadcasted_iota(jnp.int32, q.shape, 1)
    zero = jnp.zeros_like(q)
    qs = (jnp.where(lane < B_QK_DIM, q, zero), jnp.where(lane >= B_QK_DIM, q, zero))
    s_refs = (s1_s, s2_s)

    mx = [jnp.full((tq, LANES), -jnp.inf, F32)] * 2
    for c0, w in chunks:
        kc = kk_s[c0:c0 + w, :]
        for t in range(2):
            sc = lax.dot_general(qs[t], kc, (((1,), (1,)), ((), ())), preferred_element_type=F32)
            s_refs[t][:, c0:c0 + w] = sc
            for l in range(0, w, LANES):
                mx[t] = jnp.maximum(mx[t], sc[:, l:l + LANES])
    mb = [jnp.broadcast_to(jnp.max(mx[t], axis=-1, keepdims=True), (tq, LANES)) for t in range(2)]

    accs = [None, None]
    for c0, w in chunks:
        v1c = v1_s[c0:c0 + w, :]
        for t in range(2):
            p = jnp.concatenate([jnp.exp2(s_refs[t][:, c0 + l:c0 + l + LANES] - mb[t]).astype(BF16)
                                 for l in range(0, w, LANES)], axis=1)
            d = jnp.dot(p, v1c, preferred_element_type=F32)
            accs[t] = d if accs[t] is None else accs[t] + d

    lv = lam_ref[...]
    lam = (jnp.exp(jnp.sum(lv[0:1, :] * lv[1:2, :], axis=-1, keepdims=True))
           - jnp.exp(jnp.sum(lv[2:3, :] * lv[3:4, :], axis=-1, keepdims=True)) + lambda_init)
    o1 = accs[0][:, 0:HEAD_W] / accs[0][:, HEAD_W:HEAD_W + 1]
    o2 = accs[1][:, 0:HEAD_W] / accs[1][:, HEAD_W:HEAD_W + 1]
    o = o1 - lam * o2
    ms = jnp.mean(o * o, axis=-1, keepdims=True)
    y = (o * lax.rsqrt(ms + EPS) * og_ref[...]) * (1.0 - lambda_init)
    o_ref[...] = (y * sbg_ref[...].astype(F32)).astype(BF16)


def _attention(q, k, kc, v, vc, sbg, lam_vecs, og, *, tq, tk, lambda_init):
    bsz, n_lat, _ = q.shape
    n_ctx = kc.shape[1]
    qtile = pl.BlockSpec((None, tq, HEAD_W), lambda b, h, i: (b, i, h))
    whole = lambda n: pl.BlockSpec((None, n, HEAD_W), lambda b, h, i: (b, 0, h))
    const2 = lambda b, h, i: (0, 0)
    return pl.pallas_call(
        functools.partial(_attn_body, tq=tq, tk=tk, n_lat=n_lat, n_ctx=n_ctx, lambda_init=lambda_init),
        grid=(bsz, B_HEADS, n_lat // tq),
        in_specs=[qtile, whole(n_lat), whole(n_ctx), whole(n_lat), whole(n_ctx), qtile,
                  pl.BlockSpec((4, B_QK_DIM), const2), pl.BlockSpec((1, HEAD_W), const2)],
        out_specs=qtile,
        out_shape=jax.ShapeDtypeStruct((bsz, n_lat, B_WIDTH), BF16),
        scratch_shapes=[pltpu.VMEM((n_lat + n_ctx, HEAD_W), BF16),
                        pltpu.VMEM((n_lat + n_ctx, 2 * HEAD_W), BF16),
                        pltpu.VMEM((tq, n_lat + n_ctx), F32),
                        pltpu.VMEM((tq, n_lat + n_ctx), F32)],
        compiler_params=_cparams(("parallel", "parallel", "arbitrary")),
        name="diff_attn",
    )(q, k, kc, v, vc, sbg, lam_vecs, og)


def _even_out_body(ya_ref, yb_ref, wa_ref, wb_ref, x_ref, gt_ref, o_ref):
    y = (jnp.dot(ya_ref[...], wa_ref[...], preferred_element_type=F32)
         + jnp.dot(yb_ref[...], wb_ref[...], preferred_element_type=F32))
    o_ref[...] = x_ref[...] + gt_ref[...] * y


def _even_out(ya, yb, w_bf, x2d, gt, *, tm, rows_per_batch):
    m = x2d.shape[0]
    tiles_per_batch = rows_per_batch // tm
    row = lambda i: (i, 0)
    return pl.pallas_call(
        _even_out_body,
        grid=(m // tm,),
        in_specs=[pl.BlockSpec((tm, A_WIDTH), row), pl.BlockSpec((tm, B_WIDTH), row),
                  pl.BlockSpec((A_WIDTH, D_MODEL), lambda i: (0, 0)),
                  pl.BlockSpec((B_WIDTH, D_MODEL), lambda i: (1, 0)),
                  pl.BlockSpec((tm, D_MODEL), row),
                  pl.BlockSpec((None, 1, D_MODEL), lambda i: (i // tiles_per_batch, 0, 0))],
        out_specs=pl.BlockSpec((tm, D_MODEL), row),
        out_shape=jax.ShapeDtypeStruct((m, D_MODEL), F32),
        compiler_params=_cparams(("parallel",)),
        name="even_out",
    )(ya, yb, w_bf, w_bf, x2d, gt)


def _odd_in_body(x_ref, sh_ref, sc_ref, ng_ref, w_ref, y_ref, sg_ref, h_s, a_s):
    j = pl.program_id(1)

    @pl.when(j == 0)
    def _():
        h_s[...] = _norm_modulate(x_ref[...], ng_ref[...], sh_ref[...], sc_ref[...]).astype(BF16)

    acc = jnp.dot(h_s[...], w_ref[...], preferred_element_type=F32)

    @pl.when((j == 0) | (j == 2))
    def _():
        a_s[...] = acc

    @pl.when((j == 1) | (j == 3))
    def _():
        y_ref[...] = (a_s[...] * jax.nn.sigmoid(acc)).astype(BF16)

    @pl.when(j >= 4)
    def _():
        sg_ref[...] = _silu(acc).astype(BF16)


def _odd_in(x2d, sh, sc, ng, w_bf, *, tm, rows_per_batch):
    m = x2d.shape[0]
    tiles_per_batch = rows_per_batch // tm
    row = lambda i, j: (i, 0)
    batch3 = lambda i, j: (i // tiles_per_batch, 0, 0)
    wcol = lambda i, j: (0, jnp.where(j < 4, (j % 2) * 2 + j // 2, j))
    out = jax.ShapeDtypeStruct((m, D_MODEL), BF16)
    return pl.pallas_call(
        _odd_in_body,
        grid=(m // tm, 6),
        in_specs=[pl.BlockSpec((tm, D_MODEL), row),
                  pl.BlockSpec((None, 1, D_MODEL), batch3),
                  pl.BlockSpec((None, 1, D_MODEL), batch3),
                  pl.BlockSpec((1, D_MODEL), lambda i, j: (0, 0)),
                  pl.BlockSpec((D_MODEL, SEC_W), wcol)],
        out_specs=[pl.BlockSpec((tm, SEC_W), lambda i, j: (i, jnp.minimum(j // 2, 1))),
                   pl.BlockSpec((tm, SEC_W), lambda i, j: (i, jnp.maximum(j - 4, 0)))],
        out_shape=[out, out],
        scratch_shapes=[pltpu.VMEM((tm, D_MODEL), BF16), pltpu.VMEM((tm, SEC_W), F32)],
        compiler_params=_cparams(("parallel", "arbitrary")),
        name="odd_in",
    )(x2d, sh, sc, ng, w_bf)


def _odd_out_body(y_ref, yp_ref, yn_ref, sg_ref, x_ref, gt_ref, dw_ref, db_ref, lg_ref, lb_ref, w_ref,
                  o_ref, ybuf_s, conv_s, *, tm, tiles_per_batch, rb):
    i = pl.program_id(0)
    t = i % tiles_per_batch
    ybuf_s[0:HALO, :] = jnp.where(t > 0, yp_ref[...].astype(F32), 0.0)
    ybuf_s[HALO:HALO + tm, :] = y_ref[...].astype(F32)
    ybuf_s[HALO + tm:2 * HALO + tm, :] = jnp.where(t < tiles_per_batch - 1, yn_ref[...].astype(F32), 0.0)

    def col_block(cb, carry):
        c0 = pl.multiple_of(cb * LANES, LANES)
        bias = db_ref[:, pl.ds(c0, LANES)]
        for r in range(tm // rb):
            acc = jnp.broadcast_to(bias, (rb, LANES))
            for k in range(CONV_W):
                r0 = r * rb + k + HALO - CONV_PAD
                acc = acc + ybuf_s[r0:r0 + rb, pl.ds(c0, LANES)] * dw_ref[k:k + 1, pl.ds(c0, LANES)]
            conv_s[r * rb:(r + 1) * rb, pl.ds(c0, LANES)] = acc
        return carry

    lax.fori_loop(0, D_MODEL // LANES, col_block, 0)

    c = conv_s[...]
    mu = jnp.mean(c, axis=-1, keepdims=True)
    cc = c - mu
    var = jnp.mean(cc * cc, axis=-1, keepdims=True)
    z = cc * lax.rsqrt(var + EPS) * lg_ref[...] + lb_ref[...]
    u = (_silu(z) * sg_ref[...].astype(F32)).astype(BF16)
    o_ref[...] = x_ref[...] + gt_ref[...] * jnp.dot(u, w_ref[...], preferred_element_type=F32)


def _odd_out(y, sg, x2d, gt, dw_w, dw_b, ln_g, ln_b, w_bf, *, tm, rows_per_batch):
    m = x2d.shape[0]
    tiles_per_batch = rows_per_batch // tm
    hpt = tm // HALO
    last_halo = m // HALO - 1
    row = lambda i: (i, 0)
    const2 = lambda i: (0, 0)
    return pl.pallas_call(
        functools.partial(_odd_out_body, tm=tm, tiles_per_batch=tiles_per_batch, rb=64),
        grid=(m // tm,),
        in_specs=[pl.BlockSpec((tm, D_MODEL), row),
                  pl.BlockSpec((HALO, D_MODEL), lambda i: (jnp.maximum(i * hpt - 1, 0), 0)),
                  pl.BlockSpec((HALO, D_MODEL), lambda i: (jnp.minimum((i + 1) * hpt, last_halo), 0)),
                  pl.BlockSpec((tm, D_MODEL), row),
                  pl.BlockSpec((tm, D_MODEL), row),
                  pl.BlockSpec((None, 1, D_MODEL), lambda i: (i // tiles_per_batch, 0, 0)),
                  pl.BlockSpec((CONV_W + 1, D_MODEL), const2),
                  pl.BlockSpec((1, D_MODEL), const2),
                  pl.BlockSpec((1, D_MODEL), const2),
                  pl.BlockSpec((1, D_MODEL), const2),
                  pl.BlockSpec((D_MODEL, D_MODEL), const2)],
        out_specs=pl.BlockSpec((tm, D_MODEL), row),
        out_shape=jax.ShapeDtypeStruct((m, D_MODEL), F32),
        scratch_shapes=[pltpu.VMEM((tm + 2 * HALO, D_MODEL), F32), pltpu.VMEM((tm, D_MODEL), F32)],
        compiler_params=_cparams(("parallel",)),
        name="odd_out",
    )(y, y, y, sg, x2d, gt, dw_w, dw_b, ln_g, ln_b, w_bf)


def _rope_tables(n_lat):
    axis_dim = B_QK_DIM // 2
    inv = ROPE_BASE ** (-jnp.arange(0, axis_dim, 2, dtype=F32) / axis_dim)
    rows = n_lat // GRID_W
    r = jnp.repeat(jnp.arange(rows, dtype=F32), GRID_W)
    c = jnp.tile(jnp.arange(GRID_W, dtype=F32), rows)
    ar = r[:, None] * inv[None, :]
    ac = c[:, None] * inv[None, :]
    ang = jnp.concatenate([ar, ar, ac, ac], axis=-1)
    sign = jnp.where((jnp.arange(B_QK_DIM) % 32) < 16, -1.0, 1.0).astype(F32)
    cos = jnp.tile(jnp.cos(ang), (1, 2))
    sin = jnp.tile(jnp.sin(ang) * sign[None, :], (1, 2))
    return cos, sin


def kernel(x, c, ctx, c_ctx, e_norm_g, e_ada_w, e_ada_b, e_w_in, e_a_vnorm_g, e_a_ws, e_a_bs, e_b_qnorm_g, e_b_knorm_g, e_b_lambda, e_b_onorm_g, e_w_out, o_norm_g, o_ada_w, o_ada_b, o_w_in, o_dw_w, o_dw_b, o_ln_g, o_ln_b, o_w_out):
    bsz, n_lat, d = x.shape
    n_ctx = ctx.shape[1]
    x2d = x.reshape(bsz * n_lat, d)
    ctx2d = ctx.reshape(bsz * n_ctx, d)
    cond = jnp.concatenate([c, c_ctx[None, :], jnp.zeros((8 - bsz - 1, d), F32)], axis=0)
    row3 = lambda t: t.reshape(t.shape[0], 1, d)

    lambda_init = 0.8 - 0.6 * math.exp(-0.3 * 0)
    mod = _adaln(cond, e_ada_w[0], e_ada_b[0])
    sh, sc, gt = (row3(mod[:bsz, n * d:(n + 1) * d]) for n in range(3))
    csh, csc = (jnp.broadcast_to(mod[bsz:bsz + 1, n * d:(n + 1) * d], (bsz, d)).reshape(bsz, 1, d)
                for n in range(2))
    cos, sin = _rope_tables(n_lat)
    gid = jnp.arange(HEAD_W) // B_QK_DIM
    bd = jnp.where(gid[:, None] == gid[None, :], 1.0 / B_QK_DIM, 0.0).astype(BF16)
    tile2 = lambda g: jnp.tile(g, 2).reshape(1, HEAD_W)
    w_in_bf = e_w_in[0].astype(BF16)
    common = (e_norm_g[0].reshape(1, d), w_in_bf, e_a_vnorm_g[0].reshape(1, A_WIDTH),
              e_a_ws[0].astype(BF16),
              jnp.broadcast_to(e_a_bs[0][:, :, None], (A_GROUPS, CHUNK, LANES)),
              tile2(e_b_qnorm_g[0]), tile2(e_b_knorm_g[0]), cos, sin, bd)
    q_scale = (B_QK_DIM ** -0.5) * LOG2E
    ya, q, k, v, sbg = _even_in(x2d, sh, sc, *common, full=True, tm=512, rows_per_batch=n_lat,
                                q_scale=q_scale)
    kc, vc = _even_in(ctx2d, csh, csc, *common, full=False, tm=n_ctx, rows_per_batch=n_ctx,
                      q_scale=q_scale)
    seq = lambda t, n: t.reshape(bsz, n, B_WIDTH)
    yb = _attention(seq(q, n_lat), seq(k, n_lat), seq(kc, n_ctx), seq(v, n_lat), seq(vc, n_ctx),
                    seq(sbg, n_lat), e_b_lambda[0], e_b_onorm_g[0].reshape(1, HEAD_W),
                    tq=256, tk=512, lambda_init=lambda_init)
    x1 = _even_out(ya, yb.reshape(bsz * n_lat, B_WIDTH), e_w_out[0].astype(BF16), x2d, gt,
                   tm=512, rows_per_batch=n_lat)

    mod = _adaln(cond, o_ada_w[0], o_ada_b[0])
    sh, sc, gt = (row3(mod[:bsz, n * d:(n + 1) * d]) for n in range(3))
    y, sg = _odd_in(x1, sh, sc, o_norm_g[0].reshape(1, d), o_w_in[0].astype(BF16),
                    tm=512, rows_per_batch=n_lat)
    dw = jnp.concatenate([o_dw_w[0], jnp.zeros((1, d), F32)], axis=0)
    x2 = _odd_out(y, sg, x1, gt, dw, o_dw_b[0].reshape(1, d), o_ln_g[0].reshape(1, d),
                  o_ln_b[0].reshape(1, d), o_w_out[0].astype(BF16), tm=256, rows_per_batch=n_lat)
    return x2.reshape(bsz, n_lat, d)
```

```python
import functools
import math

import jax
import jax.numpy as jnp
from jax import lax
from jax.experimental import pallas as pl
from jax.experimental.pallas import tpu as pltpu

F32 = jnp.float32
BF16 = jnp.bfloat16

D_MODEL = 2048
GRID_W = 64
EPS = 1e-6
CHUNK = 128
A_WIDTH = 1024
A_GROUPS = 8
B_HEADS = 8
B_QK_DIM = 64
HEAD_W = 128
B_WIDTH = 1024
ROPE_BASE = 10000.0
CONV_W = 31
CONV_PAD = 15
SEC_W = 1024
LANES = 128
SUBLANES = 8
MXU_N = 256
HALO = 16
ROW_STRIDE = 4
VMEM_LIMIT = 56 * 1024 * 1024
LOG2E = 1.4426950408889634


def _cparams(sem):
    return pltpu.CompilerParams(dimension_semantics=sem, vmem_limit_bytes=VMEM_LIMIT)


def _resident(shape):
    return pl.BlockSpec(shape, lambda *_: (0,) * len(shape), pipeline_mode=pl.Buffered(1))


def _silu(x):
    return x * jax.nn.sigmoid(x)


def _gelu(x):
    return 0.5 * x * (1.0 + lax.erf(x * (2.0 ** -0.5)))


def _adaln_body(c_ref, w_ref, b_ref, o_ref):
    s = _silu(c_ref[...]).astype(BF16)
    o_ref[...] = jnp.dot(s, w_ref[...].astype(BF16), preferred_element_type=F32) + b_ref[...]


def _adaln(cond, w, b):
    tn = 512
    n = w.shape[1]
    return pl.pallas_call(
        _adaln_body,
        grid=(n // tn,),
        in_specs=[pl.BlockSpec((8, D_MODEL), lambda j: (0, 0)),
                  pl.BlockSpec((D_MODEL, tn), lambda j: (0, j)),
                  pl.BlockSpec((1, tn), lambda j: (0, j))],
        out_specs=pl.BlockSpec((8, tn), lambda j: (0, j)),
        out_shape=jax.ShapeDtypeStruct((8, n), F32),
        compiler_params=_cparams(("parallel",)),
        name="adaln",
    )(cond, w, b.reshape(1, n))


def _norm_modulate(x, ng, sh, sc):
    ms = jnp.mean(x * x, axis=-1, keepdims=True)
    return (x * lax.rsqrt(ms + EPS) * ng) * (1.0 + sc) + sh


def _qk_norm_rope(acc, col0, g_ref, bd_ref, cos_ref, sin_ref, out_ref, rope, scale):
    for hb in range(acc.shape[1] // HEAD_W):
        t = acc[:, hb * HEAD_W:(hb + 1) * HEAD_W]
        ms = jnp.dot((t * t).astype(BF16), bd_ref[...], preferred_element_type=F32)
        t = t * lax.rsqrt(ms + EPS) * g_ref[...]
        if rope:
            lane = lax.broadcasted_iota(jnp.int32, t.shape, 1)
            r = jnp.where((lane & 31) < 16, pltpu.roll(t, LANES - 16, 1), pltpu.roll(t, 16, 1))
            t = t * cos_ref[...] + r * sin_ref[...]
        if scale != 1.0:
            t = t * scale
        out_ref[:, col0 + hb * HEAD_W:col0 + (hb + 1) * HEAD_W] = t.astype(BF16)


def _even_in_body(x_ref, sh_ref, sc_ref, ng_ref, w_ref, vng_ref, ws_ref, bsb_ref, gq_ref, gk_ref,
                  cos_ref, sin_ref, bd_ref, *rest, full, tm, q_scale):
    if full:
        ya_ref, q_ref, k_ref, v_ref, sbg_ref, h_s, au_s, gv_s, vn_s = rest
    else:
        k_ref, v_ref, h_s = rest
    sec0 = 0 if full else 4
    col_chunks = range(0, SEC_W, MXU_N)

    h_s[...] = _norm_modulate(x_ref[...], ng_ref[...], sh_ref[...], sc_ref[...]).astype(BF16)

    def proj(sec, c0):
        w0 = (sec - sec0) * SEC_W + c0
        return jnp.dot(h_s[...], w_ref[:, w0:w0 + MXU_N], preferred_element_type=F32)

    def proj_section(sec):
        w0 = (sec - sec0) * SEC_W
        return jnp.dot(h_s[...], w_ref[:, w0:w0 + SEC_W], preferred_element_type=F32)

    if full:
        for c0 in col_chunks:
            au_s[:, c0:c0 + MXU_N] = _gelu(proj(0, c0)).astype(BF16)

        ssq = jnp.zeros((tm, LANES), F32)
        for c0 in col_chunks:
            gv = _gelu(proj(1, c0))
            gv_s[:, c0:c0 + MXU_N] = gv
            for l in range(0, MXU_N, LANES):
                ssq = ssq + gv[:, l:l + LANES] * gv[:, l:l + LANES]
        inv = lax.rsqrt(jnp.sum(ssq, axis=-1, keepdims=True) * (1.0 / A_WIDTH) + EPS)
        vn_s[...] = (gv_s[...] * inv * vng_ref[...]).astype(BF16)

        for c0 in col_chunks:
            sg = _silu(proj(2, c0))
            for r0 in range(0, tm, CHUNK):
                rows = slice(r0, r0 + CHUNK)
                for l in range(0, MXU_N, LANES):
                    g = (c0 + l) // LANES
                    cols = slice(c0 + l, c0 + l + LANES)
                    mixed = jnp.dot(ws_ref[g], vn_s[rows, cols], preferred_element_type=F32) + bsb_ref[g]
                    ya_ref[rows, cols] = (au_s[rows, cols].astype(F32) * mixed * sg[rows, l:l + LANES]).astype(BF16)

        _qk_norm_rope(proj_section(3), 0, gq_ref, bd_ref, cos_ref, sin_ref, q_ref, True, q_scale)

        for c0 in col_chunks:
            sbg_ref[:, c0:c0 + MXU_N] = _silu(proj(6, c0)).astype(BF16)

    _qk_norm_rope(proj_section(4), 0, gk_ref, bd_ref, cos_ref, sin_ref, k_ref, full, 1.0)

    for c0 in col_chunks:
        v_ref[:, c0:c0 + MXU_N] = proj(5, c0).astype(BF16)


def _even_in(x2d, sh, sc, ng, w_bf, vng, ws_bf, bsb, gq, gk, cos, sin, bd, *, full, tm, rows_per_batch,
             q_scale):
    m = x2d.shape[0]
    tiles_per_batch = rows_per_batch // tm
    row = lambda i: (i, 0)
    const2 = lambda i: (0, 0)
    const3 = lambda i: (0, 0, 0)
    batch3 = lambda i: (i // tiles_per_batch, 0, 0)
    pos = lambda i: (i % tiles_per_batch, 0)
    if full:
        w_spec = _resident(w_bf.shape)
    else:
        w_spec = pl.BlockSpec((D_MODEL, 2 * SEC_W), lambda i: (0, 2), pipeline_mode=pl.Buffered(1))
    in_specs = [
        pl.BlockSpec((tm, D_MODEL), row),
        pl.BlockSpec((None, 1, D_MODEL), batch3),
        pl.BlockSpec((None, 1, D_MODEL), batch3),
        pl.BlockSpec((1, D_MODEL), const2),
        w_spec,
        pl.BlockSpec((1, A_WIDTH), const2),
        pl.BlockSpec((A_GROUPS, CHUNK, CHUNK), const3),
        pl.BlockSpec((A_GROUPS, CHUNK, LANES), const3),
        pl.BlockSpec((1, HEAD_W), const2),
        pl.BlockSpec((1, HEAD_W), const2),
        pl.BlockSpec((tm, HEAD_W), pos),
        pl.BlockSpec((tm, HEAD_W), pos),
        pl.BlockSpec((HEAD_W, HEAD_W), const2),
    ]
    sec_out = jax.ShapeDtypeStruct((m, SEC_W), BF16)
    n_out = 5 if full else 2
    scratch = [pltpu.VMEM((tm, D_MODEL), BF16)]
    if full:
        scratch += [pltpu.VMEM((tm, A_WIDTH), BF16), pltpu.VMEM((tm, A_WIDTH), F32),
                    pltpu.VMEM((tm, A_WIDTH), BF16)]
    return pl.pallas_call(
        functools.partial(_even_in_body, full=full, tm=tm, q_scale=q_scale),
        grid=(m // tm,),
        in_specs=in_specs,
        out_specs=[pl.BlockSpec((tm, SEC_W), row)] * n_out,
        out_shape=[sec_out] * n_out,
        scratch_shapes=scratch,
        compiler_params=_cparams(("parallel",)),
        name="even_in" if full else "even_in_ctx",
    )(x2d, sh, sc, ng, w_bf, vng, ws_bf, bsb, gq, gk, cos, sin, bd)


def _attn_body(q_ref, k_ref, kc_ref, v_ref, vc_ref, sbg_ref, lam_ref, og_ref, o_ref,
               kk_s, v1_s, s_s, *, tq, tg, tk, n_lat, n_ctx, lambda_init):
    i = pl.program_id(2)
    n_tot = n_lat + n_ctx
    chunks = [(c0, min(tk, n_tot - c0)) for c0 in range(0, n_tot, tk)]

    @pl.when(i == 0)
    def _():
        kk_s[0:n_lat, :] = k_ref[...]
        kk_s[n_lat:n_lat + n_ctx, :] = kc_ref[...]
        v1_s[0:n_lat, 0:HEAD_W] = v_ref[...]
        v1_s[n_lat:n_lat + n_ctx, 0:HEAD_W] = vc_ref[...]
        v1_s[:, HEAD_W:2 * HEAD_W] = jnp.ones((n_lat + n_ctx, HEAD_W), BF16)

    q = q_ref[...]
    lane = lax.broadcasted_iota(jnp.int32, q.shape, 1)
    zero = jnp.zeros_like(q)
    qs = (jnp.where(lane < B_QK_DIM, q, zero), jnp.where(lane >= B_QK_DIM, q, zero))
    groups = [(t, r0) for r0 in range(0, tq, tg) for t in range(2)]

    def scores(g):
        t, r0 = groups[g]
        qg = qs[t][r0:r0 + tg, :]
        mx = jnp.full((tg, LANES), -jnp.inf, F32)
        for c0, w in chunks:
            sc = lax.dot_general(qg, kk_s[c0:c0 + w, :], (((1,), (1,)), ((), ())), preferred_element_type=F32)
            s_s[g * tg:(g + 1) * tg, c0:c0 + w] = sc
            for l in range(0, w, LANES):
                mx = jnp.maximum(mx, sc[:, l:l + LANES])
        return jnp.broadcast_to(jnp.max(mx, axis=-1, keepdims=True), (tg, LANES))

    def weighted_sum(g, mb):
        acc = None
        for c0, w in chunks:
            p = jnp.concatenate([jnp.exp2(s_s[g * tg:(g + 1) * tg, c0 + l:c0 + l + LANES] - mb).astype(BF16)
                                 for l in range(0, w, LANES)], axis=1)
            d = jnp.dot(p, v1_s[c0:c0 + w, :], preferred_element_type=F32)
            acc = d if acc is None else acc + d
        return acc

    n_groups = len(groups)
    accs = [None] * n_groups
    mb_prev = scores(0)
    for g in range(1, n_groups):
        mb_next = scores(g)
        accs[g - 1] = weighted_sum(g - 1, mb_prev)
        mb_prev = mb_next
    accs[n_groups - 1] = weighted_sum(n_groups - 1, mb_prev)

    lv = lam_ref[...]
    lam = (jnp.exp(jnp.sum(lv[0:1, :] * lv[1:2, :], axis=-1, keepdims=True))
           - jnp.exp(jnp.sum(lv[2:3, :] * lv[3:4, :], axis=-1, keepdims=True)) + lambda_init)
    for n, r0 in enumerate(range(0, tq, tg)):
        a1, a2 = accs[2 * n], accs[2 * n + 1]
        o = a1[:, 0:HEAD_W] / a1[:, HEAD_W:HEAD_W + 1] - lam * (a2[:, 0:HEAD_W] / a2[:, HEAD_W:HEAD_W + 1])
        ms = jnp.mean(o * o, axis=-1, keepdims=True)
        y = (o * lax.rsqrt(ms + EPS) * og_ref[...]) * (1.0 - lambda_init)
        o_ref[r0:r0 + tg, :] = (y * sbg_ref[r0:r0 + tg, :].astype(F32)).astype(BF16)


def _attention(q, k, kc, v, vc, sbg, lam_vecs, og, *, tq, tg, tk, lambda_init):
    bsz, n_lat, _ = q.shape
    n_ctx = kc.shape[1]
    qtile = pl.BlockSpec((None, tq, HEAD_W), lambda b, h, i: (b, i, h))
    whole = lambda n: pl.BlockSpec((None, n, HEAD_W), lambda b, h, i: (b, 0, h))
    const2 = lambda b, h, i: (0, 0)
    return pl.pallas_call(
        functools.partial(_attn_body, tq=tq, tg=tg, tk=tk, n_lat=n_lat, n_ctx=n_ctx, lambda_init=lambda_init),
        grid=(bsz, B_HEADS, n_lat // tq),
        in_specs=[qtile, whole(n_lat), whole(n_ctx), whole(n_lat), whole(n_ctx), qtile,
                  pl.BlockSpec((4, B_QK_DIM), const2), pl.BlockSpec((1, HEAD_W), const2)],
        out_specs=qtile,
        out_shape=jax.ShapeDtypeStruct((bsz, n_lat, B_WIDTH), BF16),
        scratch_shapes=[pltpu.VMEM((n_lat + n_ctx, HEAD_W), BF16),
                        pltpu.VMEM((n_lat + n_ctx, 2 * HEAD_W), BF16),
                        pltpu.VMEM((2 * tq, n_lat + n_ctx), F32)],
        compiler_params=_cparams(("parallel", "parallel", "arbitrary")),
        name="diff_attn",
    )(q, k, kc, v, vc, sbg, lam_vecs, og)


def _even_out_body(ya_ref, yb_ref, wa_ref, wb_ref, x_ref, gt_ref, o_ref):
    y = (jnp.dot(ya_ref[...], wa_ref[...], preferred_element_type=F32)
         + jnp.dot(yb_ref[...], wb_ref[...], preferred_element_type=F32))
    o_ref[...] = x_ref[...] + gt_ref[...] * y


def _even_out(ya, yb, w_bf, x2d, gt, *, tm, rows_per_batch):
    m = x2d.shape[0]
    tiles_per_batch = rows_per_batch // tm
    row = lambda i: (i, 0)
    return pl.pallas_call(
        _even_out_body,
        grid=(m // tm,),
        in_specs=[pl.BlockSpec((tm, A_WIDTH), row), pl.BlockSpec((tm, B_WIDTH), row),
                  pl.BlockSpec((A_WIDTH, D_MODEL), lambda i: (0, 0)),
                  pl.BlockSpec((B_WIDTH, D_MODEL), lambda i: (1, 0)),
                  pl.BlockSpec((tm, D_MODEL), row),
                  pl.BlockSpec((None, 1, D_MODEL), lambda i: (i // tiles_per_batch, 0, 0))],
        out_specs=pl.BlockSpec((tm, D_MODEL), row),
        out_shape=jax.ShapeDtypeStruct((m, D_MODEL), F32),
        compiler_params=_cparams(("parallel",)),
        name="even_out",
    )(ya, yb, w_bf, w_bf, x2d, gt)


def _odd_in_body(x_ref, sh_ref, sc_ref, ng_ref, w_ref, y_ref, sg_ref, h_s):
    h_s[...] = _norm_modulate(x_ref[...], ng_ref[...], sh_ref[...], sc_ref[...]).astype(BF16)

    def proj(c0):
        return jnp.dot(h_s[...], w_ref[:, c0:c0 + MXU_N], preferred_element_type=F32)

    for c0 in range(0, D_MODEL, MXU_N):
        y_ref[:, c0:c0 + MXU_N] = (proj(c0) * jax.nn.sigmoid(proj(D_MODEL + c0))).astype(BF16)
        sg_ref[:, c0:c0 + MXU_N] = _silu(proj(2 * D_MODEL + c0)).astype(BF16)


def _odd_in(x2d, sh, sc, ng, w_bf, *, tm, rows_per_batch):
    m = x2d.shape[0]
    tiles_per_batch = rows_per_batch // tm
    row = lambda i: (i, 0)
    batch3 = lambda i: (i // tiles_per_batch, 0, 0)
    out = jax.ShapeDtypeStruct((m, D_MODEL), BF16)
    return pl.pallas_call(
        _odd_in_body,
        grid=(m // tm,),
        in_specs=[pl.BlockSpec((tm, D_MODEL), row),
                  pl.BlockSpec((None, 1, D_MODEL), batch3),
                  pl.BlockSpec((None, 1, D_MODEL), batch3),
                  pl.BlockSpec((1, D_MODEL), lambda i: (0, 0)),
                  _resident(w_bf.shape)],
        out_specs=[pl.BlockSpec((tm, D_MODEL), row)] * 2,
        out_shape=[out, out],
        scratch_shapes=[pltpu.VMEM((tm, D_MODEL), BF16)],
        compiler_params=_cparams(("parallel",)),
        name="odd_in",
    )(x2d, sh, sc, ng, w_bf)


def _odd_out_body(y_ref, yp_ref, yn_ref, sg_ref, x_ref, gt_ref, dw_ref, db_ref, lg_ref, lb_ref, w_ref,
                  o_ref, ybuf_s, conv_s, u_s, *, tm, tiles_per_batch):
    i = pl.program_id(0)
    t = i % tiles_per_batch
    n_slabs = D_MODEL // LANES
    for cb in range(n_slabs):
        cols = slice(cb * LANES, (cb + 1) * LANES)
        ybuf_s[cb, 0:HALO, :] = jnp.where(t > 0, yp_ref[:, cols].astype(F32), 0.0)
        ybuf_s[cb, HALO:HALO + tm, :] = y_ref[:, cols].astype(F32)
        ybuf_s[cb, HALO + tm:2 * HALO + tm, :] = jnp.where(t < tiles_per_batch - 1, yn_ref[:, cols].astype(F32), 0.0)

    span = SUBLANES * ROW_STRIDE
    bases = [blk * span + s for blk in range(tm // span) for s in range(ROW_STRIDE)]
    group = 8

    def col_block(cb, carry):
        c0 = pl.multiple_of(cb * LANES, LANES)
        bias = jnp.broadcast_to(db_ref[:, pl.ds(c0, LANES)], (SUBLANES, LANES))
        for g0 in range(0, len(bases), group):
            accs = [bias] * group
            for k in range(CONV_W):
                wk = dw_ref[k, :, pl.ds(c0, LANES)]
                for n, b in enumerate(bases[g0:g0 + group]):
                    rows = pl.ds(b + k + HALO - CONV_PAD, SUBLANES, stride=ROW_STRIDE)
                    accs[n] = accs[n] + ybuf_s[cb, rows, :] * wk
            for n, b in enumerate(bases[g0:g0 + group]):
                conv_s[cb, pl.ds(b, SUBLANES, stride=ROW_STRIDE), :] = accs[n]
        return carry

    lax.fori_loop(0, n_slabs, col_block, 0)

    tot = conv_s[0]
    for cb in range(1, n_slabs):
        tot = tot + conv_s[cb]
    mu = jnp.sum(tot, axis=-1, keepdims=True) * (1.0 / D_MODEL)
    sq = jnp.zeros((tm, LANES), F32)
    for cb in range(n_slabs):
        d = conv_s[cb] - mu
        sq = sq + d * d
    inv = lax.rsqrt(jnp.sum(sq, axis=-1, keepdims=True) * (1.0 / D_MODEL) + EPS)
    for cb in range(n_slabs):
        cols = slice(cb * LANES, (cb + 1) * LANES)
        z = (conv_s[cb] - mu) * inv * lg_ref[:, cols] + lb_ref[:, cols]
        u_s[:, cols] = (_silu(z) * sg_ref[:, cols].astype(F32)).astype(BF16)
    o_ref[...] = x_ref[...] + gt_ref[...] * jnp.dot(u_s[...], w_ref[...], preferred_element_type=F32)


def _odd_out(y, sg, x2d, gt, dw8, dw_b, ln_g, ln_b, w_bf, *, tm, rows_per_batch):
    m = x2d.shape[0]
    tiles_per_batch = rows_per_batch // tm
    hpt = tm // HALO
    last_halo = m // HALO - 1
    row = lambda i: (i, 0)
    const2 = lambda i: (0, 0)
    return pl.pallas_call(
        functools.partial(_odd_out_body, tm=tm, tiles_per_batch=tiles_per_batch),
        grid=(m // tm,),
        in_specs=[pl.BlockSpec((tm, D_MODEL), row),
                  pl.BlockSpec((HALO, D_MODEL), lambda i: (jnp.maximum(i * hpt - 1, 0), 0)),
                  pl.BlockSpec((HALO, D_MODEL), lambda i: (jnp.minimum((i + 1) * hpt, last_halo), 0)),
                  pl.BlockSpec((tm, D_MODEL), row),
                  pl.BlockSpec((tm, D_MODEL), row),
                  pl.BlockSpec((None, 1, D_MODEL), lambda i: (i // tiles_per_batch, 0, 0)),
                  _resident(dw8.shape),
                  pl.BlockSpec((1, D_MODEL), const2),
                  pl.BlockSpec((1, D_MODEL), const2),
                  pl.BlockSpec((1, D_MODEL), const2),
                  _resident(w_bf.shape)],
        out_specs=pl.BlockSpec((tm, D_MODEL), row),
        out_shape=jax.ShapeDtypeStruct((m, D_MODEL), F32),
        scratch_shapes=[pltpu.VMEM((D_MODEL // LANES, tm + 2 * HALO, LANES), F32),
                        pltpu.VMEM((D_MODEL // LANES, tm, LANES), F32),
                        pltpu.VMEM((tm, D_MODEL), BF16)],
        compiler_params=_cparams(("parallel",)),
        name="odd_out",
    )(y, y, y, sg, x2d, gt, dw8, dw_b, ln_g, ln_b, w_bf)


def _rope_tables(n_lat):
    axis_dim = B_QK_DIM // 2
    inv = ROPE_BASE ** (-jnp.arange(0, axis_dim, 2, dtype=F32) / axis_dim)
    rows = n_lat // GRID_W
    r = jnp.repeat(jnp.arange(rows, dtype=F32), GRID_W)
    c = jnp.tile(jnp.arange(GRID_W, dtype=F32), rows)
    ar = r[:, None] * inv[None, :]
    ac = c[:, None] * inv[None, :]
    ang = jnp.concatenate([ar, ar, ac, ac], axis=-1)
    sign = jnp.where((jnp.arange(B_QK_DIM) % 32) < 16, -1.0, 1.0).astype(F32)
    cos = jnp.tile(jnp.cos(ang), (1, 2))
    sin = jnp.tile(jnp.sin(ang) * sign[None, :], (1, 2))
    return cos, sin


def kernel(x, c, ctx, c_ctx, e_norm_g, e_ada_w, e_ada_b, e_w_in, e_a_vnorm_g, e_a_ws, e_a_bs, e_b_qnorm_g, e_b_knorm_g, e_b_lambda, e_b_onorm_g, e_w_out, o_norm_g, o_ada_w, o_ada_b, o_w_in, o_dw_w, o_dw_b, o_ln_g, o_ln_b, o_w_out):
    bsz, n_lat, d = x.shape
    n_ctx = ctx.shape[1]
    x2d = x.reshape(bsz * n_lat, d)
    ctx2d = ctx.reshape(bsz * n_ctx, d)
    cond = jnp.concatenate([c, c_ctx[None, :], jnp.zeros((8 - bsz - 1, d), F32)], axis=0)
    row3 = lambda t: t.reshape(t.shape[0], 1, d)

    lambda_init = 0.8 - 0.6 * math.exp(-0.3 * 0)
    mod = _adaln(cond, e_ada_w[0], e_ada_b[0])
    sh, sc, gt = (row3(mod[:bsz, n * d:(n + 1) * d]) for n in range(3))
    csh, csc = (jnp.broadcast_to(mod[bsz:bsz + 1, n * d:(n + 1) * d], (bsz, d)).reshape(bsz, 1, d)
                for n in range(2))
    cos, sin = _rope_tables(n_lat)
    gid = jnp.arange(HEAD_W) // B_QK_DIM
    bd = jnp.where(gid[:, None] == gid[None, :], 1.0 / B_QK_DIM, 0.0).astype(BF16)
    tile2 = lambda g: jnp.tile(g, 2).reshape(1, HEAD_W)
    w_in_bf = e_w_in[0].astype(BF16)
    common = (e_norm_g[0].reshape(1, d), w_in_bf, e_a_vnorm_g[0].reshape(1, A_WIDTH),
              e_a_ws[0].astype(BF16),
              jnp.broadcast_to(e_a_bs[0][:, :, None], (A_GROUPS, CHUNK, LANES)),
              tile2(e_b_qnorm_g[0]), tile2(e_b_knorm_g[0]), cos, sin, bd)
    q_scale = (B_QK_DIM ** -0.5) * LOG2E
    ya, q, k, v, sbg = _even_in(x2d, sh, sc, *common, full=True, tm=256, rows_per_batch=n_lat,
                                q_scale=q_scale)
    kc, vc = _even_in(ctx2d, csh, csc, *common, full=False, tm=n_ctx, rows_per_batch=n_ctx,
                      q_scale=q_scale)
    seq = lambda t, n: t.reshape(bsz, n, B_WIDTH)
    yb = _attention(seq(q, n_lat), seq(k, n_lat), seq(kc, n_ctx), seq(v, n_lat), seq(vc, n_ctx),
                    seq(sbg, n_lat), e_b_lambda[0], e_b_onorm_g[0].reshape(1, HEAD_W),
                    tq=1024, tg=256, tk=512, lambda_init=lambda_init)
    x1 = _even_out(ya, yb.reshape(bsz * n_lat, B_WIDTH), e_w_out[0].astype(BF16), x2d, gt,
                   tm=512, rows_per_batch=n_lat)

    mod = _adaln(cond, o_ada_w[0], o_ada_b[0])
    sh, sc, gt = (row3(mod[:bsz, n * d:(n + 1) * d]) for n in range(3))
    y, sg = _odd_in(x1, sh, sc, o_norm_g[0].reshape(1, d), o_w_in[0].astype(BF16),
                    tm=256, rows_per_batch=n_lat)
    dw8 = jnp.broadcast_to(o_dw_w[0][:, None, :], (CONV_W, SUBLANES, d))
    x2 = _odd_out(y, sg, x1, gt, dw8, o_dw_b[0].reshape(1, d), o_ln_g[0].reshape(1, d),
                  o_ln_b[0].reshape(1, d), o_w_out[0].astype(BF16), tm=256, rows_per_batch=n_lat)
    return x2.reshape(bsz, n_lat, d)
```

```python
import functools
import math

import jax
import jax.numpy as jnp
from jax import lax
from jax.experimental import pallas as pl
from jax.experimental.pallas import tpu as pltpu

F32 = jnp.float32
BF16 = jnp.bfloat16

D_MODEL = 2048
GRID_W = 64
EPS = 1e-6
CHUNK = 128
A_WIDTH = 1024
A_GROUPS = 8
B_HEADS = 8
B_QK_DIM = 64
HEAD_W = 128
B_WIDTH = 1024
ROPE_BASE = 10000.0
CONV_W = 31
CONV_PAD = 15
SEC_W = 1024
LANES = 128
SUBLANES = 8
BF16_ROWS = 16
MXU_N = 256
HALO = 16
ROW_STRIDE = 4
VMEM_LIMIT = 56 * 1024 * 1024
LOG2E = 1.4426950408889634


def _cparams(sem):
    return pltpu.CompilerParams(dimension_semantics=sem, vmem_limit_bytes=VMEM_LIMIT)


def _resident(shape):
    return pl.BlockSpec(shape, lambda *_: (0,) * len(shape), pipeline_mode=pl.Buffered(1))


def _silu(x):
    return x * jax.nn.sigmoid(x)


def _gelu(x):
    return 0.5 * x * (1.0 + lax.erf(x * (2.0 ** -0.5)))


def _adaln_body(c_ref, w_ref, b_ref, o_ref):
    s = _silu(c_ref[...]).astype(BF16)
    o_ref[...] = jnp.dot(s, w_ref[...].astype(BF16), preferred_element_type=F32) + b_ref[...]


def _adaln(cond, w, b):
    tn = 512
    n = w.shape[1]
    return pl.pallas_call(
        _adaln_body,
        grid=(n // tn,),
        in_specs=[pl.BlockSpec((8, D_MODEL), lambda j: (0, 0)),
                  pl.BlockSpec((D_MODEL, tn), lambda j: (0, j)),
                  pl.BlockSpec((1, tn), lambda j: (0, j))],
        out_specs=pl.BlockSpec((8, tn), lambda j: (0, j)),
        out_shape=jax.ShapeDtypeStruct((8, n), F32),
        compiler_params=_cparams(("parallel",)),
        name="adaln",
    )(cond, w, b.reshape(1, n))


def _norm_modulate(x, ng, sh, sc):
    ms = jnp.mean(x * x, axis=-1, keepdims=True)
    return (x * lax.rsqrt(ms + EPS) * ng) * (1.0 + sc) + sh


def _qk_norm_rope(acc, col0, g_ref, bd_ref, cos_ref, sin_ref, out_ref, rope, scale):
    for hb in range(acc.shape[1] // HEAD_W):
        t = acc[:, hb * HEAD_W:(hb + 1) * HEAD_W]
        ms = jnp.dot((t * t).astype(BF16), bd_ref[...], preferred_element_type=F32)
        t = t * lax.rsqrt(ms + EPS) * g_ref[...]
        if rope:
            lane = lax.broadcasted_iota(jnp.int32, t.shape, 1)
            r = jnp.where((lane & 31) < 16, pltpu.roll(t, LANES - 16, 1), pltpu.roll(t, 16, 1))
            t = t * cos_ref[...] + r * sin_ref[...]
        if scale != 1.0:
            t = t * scale
        out_ref[:, col0 + hb * HEAD_W:col0 + (hb + 1) * HEAD_W] = t.astype(BF16)


def _even_in_body(x_ref, sh_ref, sc_ref, ng_ref, w_ref, vng_ref, ws_ref, bsb_ref, gq_ref, gk_ref,
                  cos_ref, sin_ref, bd_ref, *rest, full, tm, q_scale):
    if full:
        ya_ref, q_ref, k_ref, v_ref, sbg_ref, h_s, au_s, gv_s, vn_s = rest
    else:
        k_ref, v_ref, h_s = rest
    sec0 = 0 if full else 4
    col_chunks = range(0, SEC_W, MXU_N)

    h_s[...] = _norm_modulate(x_ref[...], ng_ref[...], sh_ref[...], sc_ref[...]).astype(BF16)

    def proj(sec, c0):
        w0 = (sec - sec0) * SEC_W + c0
        return jnp.dot(h_s[...], w_ref[:, w0:w0 + MXU_N], preferred_element_type=F32)

    def proj_section(sec):
        w0 = (sec - sec0) * SEC_W
        return jnp.dot(h_s[...], w_ref[:, w0:w0 + SEC_W], preferred_element_type=F32)

    if full:
        for c0 in col_chunks:
            au_s[:, c0:c0 + MXU_N] = _gelu(proj(0, c0)).astype(BF16)

        ssq = jnp.zeros((tm, LANES), F32)
        for c0 in col_chunks:
            gv = _gelu(proj(1, c0))
            gv_s[:, c0:c0 + MXU_N] = gv
            for l in range(0, MXU_N, LANES):
                ssq = ssq + gv[:, l:l + LANES] * gv[:, l:l + LANES]
        inv = lax.rsqrt(jnp.sum(ssq, axis=-1, keepdims=True) * (1.0 / A_WIDTH) + EPS)
        vn_s[...] = (gv_s[...] * inv * vng_ref[...]).astype(BF16)

        for c0 in col_chunks:
            sg = _silu(proj(2, c0))
            for r0 in range(0, tm, CHUNK):
                rows = slice(r0, r0 + CHUNK)
                for l in range(0, MXU_N, LANES):
                    g = (c0 + l) // LANES
                    cols = slice(c0 + l, c0 + l + LANES)
                    mixed = jnp.dot(ws_ref[g], vn_s[rows, cols], preferred_element_type=F32) + bsb_ref[g]
                    ya_ref[rows, cols] = (au_s[rows, cols].astype(F32) * mixed * sg[rows, l:l + LANES]).astype(BF16)

        _qk_norm_rope(proj_section(3), 0, gq_ref, bd_ref, cos_ref, sin_ref, q_ref, True, q_scale)

        for c0 in col_chunks:
            sbg_ref[:, c0:c0 + MXU_N] = _silu(proj(6, c0)).astype(BF16)

    _qk_norm_rope(proj_section(4), 0, gk_ref, bd_ref, cos_ref, sin_ref, k_ref, full, 1.0)

    for c0 in col_chunks:
        v_ref[:, c0:c0 + MXU_N] = proj(5, c0).astype(BF16)


def _even_in(x2d, sh, sc, ng, w_bf, vng, ws_bf, bsb, gq, gk, cos, sin, bd, *, full, tm, rows_per_batch,
             q_scale):
    m = x2d.shape[0]
    tiles_per_batch = rows_per_batch // tm
    row = lambda i: (i, 0)
    const2 = lambda i: (0, 0)
    const3 = lambda i: (0, 0, 0)
    batch3 = lambda i: (i // tiles_per_batch, 0, 0)
    pos = lambda i: (i % tiles_per_batch, 0)
    if full:
        w_spec = _resident(w_bf.shape)
    else:
        w_spec = pl.BlockSpec((D_MODEL, 2 * SEC_W), lambda i: (0, 2), pipeline_mode=pl.Buffered(1))
    in_specs = [
        pl.BlockSpec((tm, D_MODEL), row),
        pl.BlockSpec((None, 1, D_MODEL), batch3),
        pl.BlockSpec((None, 1, D_MODEL), batch3),
        pl.BlockSpec((1, D_MODEL), const2),
        w_spec,
        pl.BlockSpec((1, A_WIDTH), const2),
        pl.BlockSpec((A_GROUPS, CHUNK, CHUNK), const3),
        pl.BlockSpec((A_GROUPS, CHUNK, LANES), const3),
        pl.BlockSpec((1, HEAD_W), const2),
        pl.BlockSpec((1, HEAD_W), const2),
        pl.BlockSpec((tm, HEAD_W), pos),
        pl.BlockSpec((tm, HEAD_W), pos),
        pl.BlockSpec((HEAD_W, HEAD_W), const2),
    ]
    sec_out = jax.ShapeDtypeStruct((m, SEC_W), BF16)
    n_out = 5 if full else 2
    scratch = [pltpu.VMEM((tm, D_MODEL), BF16)]
    if full:
        scratch += [pltpu.VMEM((tm, A_WIDTH), BF16), pltpu.VMEM((tm, A_WIDTH), F32),
                    pltpu.VMEM((tm, A_WIDTH), BF16)]
    return pl.pallas_call(
        functools.partial(_even_in_body, full=full, tm=tm, q_scale=q_scale),
        grid=(m // tm,),
        in_specs=in_specs,
        out_specs=[pl.BlockSpec((tm, SEC_W), row)] * n_out,
        out_shape=[sec_out] * n_out,
        scratch_shapes=scratch,
        compiler_params=_cparams(("parallel",)),
        name="even_in" if full else "even_in_ctx",
    )(x2d, sh, sc, ng, w_bf, vng, ws_bf, bsb, gq, gk, cos, sin, bd)


def _attn_body(q_ref, k_ref, kc_ref, v_ref, vc_ref, sbg_ref, lam_ref, og_ref, o_ref,
               kk_s, v1t_s, s_s, *, tq, tg, tk, n_lat, n_ctx, lambda_init):
    i = pl.program_id(2)
    n_tot = n_lat + n_ctx
    chunks = [(c0, min(tk, n_tot - c0)) for c0 in range(0, n_tot, tk)]

    @pl.when(i == 0)
    def _():
        kk_s[0:n_lat, :] = k_ref[...]
        kk_s[n_lat:n_tot, :] = kc_ref[...]
        for c0, w in chunks:
            src, r0 = (v_ref, c0) if c0 < n_lat else (vc_ref, c0 - n_lat)
            v1t_s[0:HEAD_W, c0:c0 + w] = src[r0:r0 + w, :].astype(F32).T.astype(BF16)
        v1t_s[HEAD_W:HEAD_W + BF16_ROWS, :] = jnp.ones((BF16_ROWS, n_tot), BF16)

    qt = q_ref[...].astype(F32).T
    row = lax.broadcasted_iota(jnp.int32, qt.shape, 0)
    qts = (jnp.where(row < B_QK_DIM, qt, 0.0).astype(BF16), jnp.where(row >= B_QK_DIM, qt, 0.0).astype(BF16))
    groups = [(t, r0) for r0 in range(0, tq, tg) for t in range(2)]

    def scores(g):
        t, r0 = groups[g]
        qg = qts[t][:, r0:r0 + tg]
        mx = jnp.full((1, tg), -jnp.inf, F32)
        for c0, w in chunks:
            sc = jnp.dot(kk_s[c0:c0 + w, :], qg, preferred_element_type=F32)
            s_s[g * n_tot + c0:g * n_tot + c0 + w, :] = sc
            mx = jnp.maximum(mx, jnp.max(sc, axis=0, keepdims=True))
        return mx

    def weighted_sum(g, mx):
        acc = None
        for c0, w in chunks:
            p = jnp.exp2(s_s[g * n_tot + c0:g * n_tot + c0 + w, :] - mx).astype(BF16)
            d = jnp.dot(v1t_s[:, c0:c0 + w], p, preferred_element_type=F32)
            acc = d if acc is None else acc + d
        return acc

    n_groups = len(groups)
    accs = [None] * n_groups
    mx_prev = scores(0)
    for g in range(1, n_groups):
        mx_next = scores(g)
        accs[g - 1] = weighted_sum(g - 1, mx_prev)
        mx_prev = mx_next
    accs[n_groups - 1] = weighted_sum(n_groups - 1, mx_prev)

    lv = lam_ref[...]
    lam = (jnp.exp(jnp.sum(lv[0:1, :] * lv[1:2, :], axis=-1, keepdims=True))
           - jnp.exp(jnp.sum(lv[2:3, :] * lv[3:4, :], axis=-1, keepdims=True)) + lambda_init)
    for n, r0 in enumerate(range(0, tq, tg)):
        a1, a2 = accs[2 * n], accs[2 * n + 1]
        ot = (a1[0:HEAD_W, :] / a1[HEAD_W:HEAD_W + 1, :]
              - lam * (a2[0:HEAD_W, :] / a2[HEAD_W:HEAD_W + 1, :]))
        ms = jnp.mean(ot * ot, axis=0, keepdims=True)
        o = (ot * lax.rsqrt(ms + EPS)).T
        y = (o * og_ref[...]) * (1.0 - lambda_init)
        o_ref[r0:r0 + tg, :] = (y * sbg_ref[r0:r0 + tg, :].astype(F32)).astype(BF16)


def _attention(q, k, kc, v, vc, sbg, lam_vecs, og, *, tq, tg, tk, lambda_init):
    bsz, n_lat, _ = q.shape
    n_ctx = kc.shape[1]
    n_tot = n_lat + n_ctx
    qtile = pl.BlockSpec((None, tq, HEAD_W), lambda b, h, i: (b, i, h))
    whole = lambda n: pl.BlockSpec((None, n, HEAD_W), lambda b, h, i: (b, 0, h))
    const2 = lambda b, h, i: (0, 0)
    return pl.pallas_call(
        functools.partial(_attn_body, tq=tq, tg=tg, tk=tk, n_lat=n_lat, n_ctx=n_ctx, lambda_init=lambda_init),
        grid=(bsz, B_HEADS, n_lat // tq),
        in_specs=[qtile, whole(n_lat), whole(n_ctx), whole(n_lat), whole(n_ctx), qtile,
                  pl.BlockSpec((4, B_QK_DIM), const2), pl.BlockSpec((1, HEAD_W), const2)],
        out_specs=qtile,
        out_shape=jax.ShapeDtypeStruct((bsz, n_lat, B_WIDTH), BF16),
        scratch_shapes=[pltpu.VMEM((n_tot, HEAD_W), BF16),
                        pltpu.VMEM((HEAD_W + BF16_ROWS, n_tot), BF16),
                        pltpu.VMEM((2 * (tq // tg) * n_tot, tg), F32)],
        compiler_params=_cparams(("parallel", "parallel", "arbitrary")),
        name="diff_attn",
    )(q, k, kc, v, vc, sbg, lam_vecs, og)


def _even_out_body(ya_ref, yb_ref, wa_ref, wb_ref, x_ref, gt_ref, o_ref):
    y = (jnp.dot(ya_ref[...], wa_ref[...], preferred_element_type=F32)
         + jnp.dot(yb_ref[...], wb_ref[...], preferred_element_type=F32))
    o_ref[...] = x_ref[...] + gt_ref[...] * y


def _even_out(ya, yb, w_bf, x2d, gt, *, tm, rows_per_batch):
    m = x2d.shape[0]
    tiles_per_batch = rows_per_batch // tm
    row = lambda i: (i, 0)
    return pl.pallas_call(
        _even_out_body,
        grid=(m // tm,),
        in_specs=[pl.BlockSpec((tm, A_WIDTH), row), pl.BlockSpec((tm, B_WIDTH), row),
                  pl.BlockSpec((A_WIDTH, D_MODEL), lambda i: (0, 0)),
                  pl.BlockSpec((B_WIDTH, D_MODEL), lambda i: (1, 0)),
                  pl.BlockSpec((tm, D_MODEL), row),
                  pl.BlockSpec((None, 1, D_MODEL), lambda i: (i // tiles_per_batch, 0, 0))],
        out_specs=pl.BlockSpec((tm, D_MODEL), row),
        out_shape=jax.ShapeDtypeStruct((m, D_MODEL), F32),
        compiler_params=_cparams(("parallel",)),
        name="even_out",
    )(ya, yb, w_bf, w_bf, x2d, gt)


def _odd_in_body(x_ref, sh_ref, sc_ref, ng_ref, w_ref, y_ref, sg_ref, h_s):
    h_s[...] = _norm_modulate(x_ref[...], ng_ref[...], sh_ref[...], sc_ref[...]).astype(BF16)

    def proj(c0):
        return jnp.dot(h_s[...], w_ref[:, c0:c0 + MXU_N], preferred_element_type=F32)

    for c0 in range(0, D_MODEL, MXU_N):
        y_ref[:, c0:c0 + MXU_N] = (proj(c0) * jax.nn.sigmoid(proj(D_MODEL + c0))).astype(BF16)
        sg_ref[:, c0:c0 + MXU_N] = _silu(proj(2 * D_MODEL + c0)).astype(BF16)


def _odd_in(x2d, sh, sc, ng, w_bf, *, tm, rows_per_batch):
    m = x2d.shape[0]
    tiles_per_batch = rows_per_batch // tm
    row = lambda i: (i, 0)
    batch3 = lambda i: (i // tiles_per_batch, 0, 0)
    out = jax.ShapeDtypeStruct((m, D_MODEL), BF16)
    return pl.pallas_call(
        _odd_in_body,
        grid=(m // tm,),
        in_specs=[pl.BlockSpec((tm, D_MODEL), row),
                  pl.BlockSpec((None, 1, D_MODEL), batch3),
                  pl.BlockSpec((None, 1, D_MODEL), batch3),
                  pl.BlockSpec((1, D_MODEL), lambda i: (0, 0)),
                  _resident(w_bf.shape)],
        out_specs=[pl.BlockSpec((tm, D_MODEL), row)] * 2,
        out_shape=[out, out],
        scratch_shapes=[pltpu.VMEM((tm, D_MODEL), BF16)],
        compiler_params=_cparams(("parallel",)),
        name="odd_in",
    )(x2d, sh, sc, ng, w_bf)


def _odd_out_body(y_ref, yp_ref, yn_ref, sg_ref, x_ref, gt_ref, dw_ref, db_ref, lg_ref, lb_ref, w_ref,
                  o_ref, ybuf_s, conv_s, u_s, *, tm, tiles_per_batch):
    i = pl.program_id(0)
    t = i % tiles_per_batch
    n_slabs = D_MODEL // LANES
    for cb in range(n_slabs):
        cols = slice(cb * LANES, (cb + 1) * LANES)
        ybuf_s[cb, 0:HALO, :] = jnp.where(t > 0, yp_ref[:, cols].astype(F32), 0.0)
        ybuf_s[cb, HALO:HALO + tm, :] = y_ref[:, cols].astype(F32)
        ybuf_s[cb, HALO + tm:2 * HALO + tm, :] = jnp.where(t < tiles_per_batch - 1, yn_ref[:, cols].astype(F32), 0.0)

    span = SUBLANES * ROW_STRIDE
    bases = [blk * span + s for blk in range(tm // span) for s in range(ROW_STRIDE)]
    group = 8

    def col_block(cb, carry):
        c0 = pl.multiple_of(cb * LANES, LANES)
        bias = jnp.broadcast_to(db_ref[:, pl.ds(c0, LANES)], (SUBLANES, LANES))
        for g0 in range(0, len(bases), group):
            accs = [bias] * group
            for k in range(CONV_W):
                wk = dw_ref[k, :, pl.ds(c0, LANES)]
                for n, b in enumerate(bases[g0:g0 + group]):
                    rows = pl.ds(b + k + HALO - CONV_PAD, SUBLANES, stride=ROW_STRIDE)
                    accs[n] = accs[n] + ybuf_s[cb, rows, :] * wk
            for n, b in enumerate(bases[g0:g0 + group]):
                conv_s[cb, pl.ds(b, SUBLANES, stride=ROW_STRIDE), :] = accs[n]
        return carry

    lax.fori_loop(0, n_slabs, col_block, 0)

    tot = conv_s[0]
    for cb in range(1, n_slabs):
        tot = tot + conv_s[cb]
    mu = jnp.sum(tot, axis=-1, keepdims=True) * (1.0 / D_MODEL)
    sq = jnp.zeros((tm, LANES), F32)
    for cb in range(n_slabs):
        d = conv_s[cb] - mu
        sq = sq + d * d
    inv = lax.rsqrt(jnp.sum(sq, axis=-1, keepdims=True) * (1.0 / D_MODEL) + EPS)
    for cb in range(n_slabs):
        cols = slice(cb * LANES, (cb + 1) * LANES)
        z = (conv_s[cb] - mu) * inv * lg_ref[:, cols] + lb_ref[:, cols]
        u_s[:, cols] = (_silu(z) * sg_ref[:, cols].astype(F32)).astype(BF16)
    o_ref[...] = x_ref[...] + gt_ref[...] * jnp.dot(u_s[...], w_ref[...], preferred_element_type=F32)


def _odd_out(y, sg, x2d, gt, dw8, dw_b, ln_g, ln_b, w_bf, *, tm, rows_per_batch):
    m = x2d.shape[0]
    tiles_per_batch = rows_per_batch // tm
    hpt = tm // HALO
    last_halo = m // HALO - 1
    row = lambda i: (i, 0)
    const2 = lambda i: (0, 0)
    return pl.pallas_call(
        functools.partial(_odd_out_body, tm=tm, tiles_per_batch=tiles_per_batch),
        grid=(m // tm,),
        in_specs=[pl.BlockSpec((tm, D_MODEL), row),
                  pl.BlockSpec((HALO, D_MODEL), lambda i: (jnp.maximum(i * hpt - 1, 0), 0)),
                  pl.BlockSpec((HALO, D_MODEL), lambda i: (jnp.minimum((i + 1) * hpt, last_halo), 0)),
                  pl.BlockSpec((tm, D_MODEL), row),
                  pl.BlockSpec((tm, D_MODEL), row),
                  pl.BlockSpec((None, 1, D_MODEL), lambda i: (i // tiles_per_batch, 0, 0)),
                  _resident(dw8.shape),
                  pl.BlockSpec((1, D_MODEL), const2),
                  pl.BlockSpec((1, D_MODEL), const2),
                  pl.BlockSpec((1, D_MODEL), const2),
                  _resident(w_bf.shape)],
        out_specs=pl.BlockSpec((tm, D_MODEL), row),
        out_shape=jax.ShapeDtypeStruct((m, D_MODEL), F32),
        scratch_shapes=[pltpu.VMEM((D_MODEL // LANES, tm + 2 * HALO, LANES), F32),
                        pltpu.VMEM((D_MODEL // LANES, tm, LANES), F32),
                        pltpu.VMEM((tm, D_MODEL), BF16)],
        compiler_params=_cparams(("parallel",)),
        name="odd_out",
    )(y, y, y, sg, x2d, gt, dw8, dw_b, ln_g, ln_b, w_bf)


def _rope_tables(n_lat):
    axis_dim = B_QK_DIM // 2
    inv = ROPE_BASE ** (-jnp.arange(0, axis_dim, 2, dtype=F32) / axis_dim)
    rows = n_lat // GRID_W
    r = jnp.repeat(jnp.arange(rows, dtype=F32), GRID_W)
    c = jnp.tile(jnp.arange(GRID_W, dtype=F32), rows)
    ar = r[:, None] * inv[None, :]
    ac = c[:, None] * inv[None, :]
    ang = jnp.concatenate([ar, ar, ac, ac], axis=-1)
    sign = jnp.where((jnp.arange(B_QK_DIM) % 32) < 16, -1.0, 1.0).astype(F32)
    cos = jnp.tile(jnp.cos(ang), (1, 2))
    sin = jnp.tile(jnp.sin(ang) * sign[None, :], (1, 2))
    return cos, sin


def kernel(x, c, ctx, c_ctx, e_norm_g, e_ada_w, e_ada_b, e_w_in, e_a_vnorm_g, e_a_ws, e_a_bs, e_b_qnorm_g, e_b_knorm_g, e_b_lambda, e_b_onorm_g, e_w_out, o_norm_g, o_ada_w, o_ada_b, o_w_in, o_dw_w, o_dw_b, o_ln_g, o_ln_b, o_w_out):
    bsz, n_lat, d = x.shape
    n_ctx = ctx.shape[1]
    x2d = x.reshape(bsz * n_lat, d)
    ctx2d = ctx.reshape(bsz * n_ctx, d)
    cond = jnp.concatenate([c, c_ctx[None, :], jnp.zeros((8 - bsz - 1, d), F32)], axis=0)
    row3 = lambda t: t.reshape(t.shape[0], 1, d)

    lambda_init = 0.8 - 0.6 * math.exp(-0.3 * 0)
    mod = _adaln(cond, e_ada_w[0], e_ada_b[0])
    sh, sc, gt = (row3(mod[:bsz, n * d:(n + 1) * d]) for n in range(3))
    csh, csc = (jnp.broadcast_to(mod[bsz:bsz + 1, n * d:(n + 1) * d], (bsz, d)).reshape(bsz, 1, d)
                for n in range(2))
    cos, sin = _rope_tables(n_lat)
    gid = jnp.arange(HEAD_W) // B_QK_DIM
    bd = jnp.where(gid[:, None] == gid[None, :], 1.0 / B_QK_DIM, 0.0).astype(BF16)
    tile2 = lambda g: jnp.tile(g, 2).reshape(1, HEAD_W)
    w_in_bf = e_w_in[0].astype(BF16)
    common = (e_norm_g[0].reshape(1, d), w_in_bf, e_a_vnorm_g[0].reshape(1, A_WIDTH),
              e_a_ws[0].astype(BF16),
              jnp.broadcast_to(e_a_bs[0][:, :, None], (A_GROUPS, CHUNK, LANES)),
              tile2(e_b_qnorm_g[0]), tile2(e_b_knorm_g[0]), cos, sin, bd)
    q_scale = (B_QK_DIM ** -0.5) * LOG2E
    ya, q, k, v, sbg = _even_in(x2d, sh, sc, *common, full=True, tm=256, rows_per_batch=n_lat,
                                q_scale=q_scale)
    kc, vc = _even_in(ctx2d, csh, csc, *common, full=False, tm=n_ctx, rows_per_batch=n_ctx,
                      q_scale=q_scale)
    seq = lambda t, n: t.reshape(bsz, n, B_WIDTH)
    yb = _attention(seq(q, n_lat), seq(k, n_lat), seq(kc, n_ctx), seq(v, n_lat), seq(vc, n_ctx),
                    seq(sbg, n_lat), e_b_lambda[0], e_b_onorm_g[0].reshape(1, HEAD_W),
                    tq=1024, tg=256, tk=512, lambda_init=lambda_init)
    x1 = _even_out(ya, yb.reshape(bsz * n_lat, B_WIDTH), e_w_out[0].astype(BF16), x2d, gt,
                   tm=512, rows_per_batch=n_lat)

    mod = _adaln(cond, o_ada_w[0], o_ada_b[0])
    sh, sc, gt = (row3(mod[:bsz, n * d:(n + 1) * d]) for n in range(3))
    y, sg = _odd_in(x1, sh, sc, o_norm_g[0].reshape(1, d), o_w_in[0].astype(BF16),
                    tm=256, rows_per_batch=n_lat)
    dw8 = jnp.broadcast_to(o_dw_w[0][:, None, :], (CONV_W, SUBLANES, d))
    x2 = _odd_out(y, sg, x1, gt, dw8, o_dw_b[0].reshape(1, d), o_ln_g[0].reshape(1, d),
                  o_ln_b[0].reshape(1, d), o_w_out[0].astype(BF16), tm=256, rows_per_batch=n_lat)
    return x2.reshape(bsz, n_lat, d)
```

```python
import functools
import math

import jax
import jax.numpy as jnp
from jax import lax
from jax.experimental import pallas as pl
from jax.experimental.pallas import tpu as pltpu

F32 = jnp.float32
BF16 = jnp.bfloat16

D_MODEL = 2048
GRID_W = 64
EPS = 1e-6
CHUNK = 128
A_WIDTH = 1024
A_GROUPS = 8
B_HEADS = 8
B_QK_DIM = 64
HEAD_W = 128
B_WIDTH = 1024
ROPE_BASE = 10000.0
CONV_W = 31
CONV_PAD = 15
SEC_W = 1024
LANES = 128
SUBLANES = 8
BF16_ROWS = 16
MXU_N = 256
HALO = 16
ROW_STRIDE = 4
S_SLOTS = 3
VMEM_LIMIT = 56 * 1024 * 1024
LOG2E = 1.4426950408889634


def _cparams(sem):
    return pltpu.CompilerParams(dimension_semantics=sem, vmem_limit_bytes=VMEM_LIMIT)


def _resident(shape):
    return pl.BlockSpec(shape, lambda *_: (0,) * len(shape), pipeline_mode=pl.Buffered(1))


def _silu(x):
    return x * jax.nn.sigmoid(x)


def _gelu(x):
    return 0.5 * x * (1.0 + lax.erf(x * (2.0 ** -0.5)))


def _adaln_body(c_ref, w_ref, b_ref, o_ref):
    s = _silu(c_ref[...]).astype(BF16)
    o_ref[...] = jnp.dot(s, w_ref[...].astype(BF16), preferred_element_type=F32) + b_ref[...]


def _adaln(cond, w, b):
    tn = 512
    n = w.shape[1]
    return pl.pallas_call(
        _adaln_body,
        grid=(n // tn,),
        in_specs=[pl.BlockSpec((8, D_MODEL), lambda j: (0, 0)),
                  pl.BlockSpec((D_MODEL, tn), lambda j: (0, j)),
                  pl.BlockSpec((1, tn), lambda j: (0, j))],
        out_specs=pl.BlockSpec((8, tn), lambda j: (0, j)),
        out_shape=jax.ShapeDtypeStruct((8, n), F32),
        compiler_params=_cparams(("parallel",)),
        name="adaln",
    )(cond, w, b.reshape(1, n))


def _norm_modulate(x, ng, sh, sc):
    ms = jnp.mean(x * x, axis=-1, keepdims=True)
    return (x * lax.rsqrt(ms + EPS) * ng) * (1.0 + sc) + sh


def _qk_norm_rope(acc, col0, g_ref, bd_ref, cos_ref, sin_ref, out_ref, rope, scale):
    for hb in range(acc.shape[1] // HEAD_W):
        t = acc[:, hb * HEAD_W:(hb + 1) * HEAD_W]
        ms = jnp.dot((t * t).astype(BF16), bd_ref[...], preferred_element_type=F32)
        t = t * lax.rsqrt(ms + EPS) * g_ref[...]
        if rope:
            lane = lax.broadcasted_iota(jnp.int32, t.shape, 1)
            r = jnp.where((lane & 31) < 16, pltpu.roll(t, LANES - 16, 1), pltpu.roll(t, 16, 1))
            t = t * cos_ref[...] + r * sin_ref[...]
        if scale != 1.0:
            t = t * scale
        out_ref[:, col0 + hb * HEAD_W:col0 + (hb + 1) * HEAD_W] = t.astype(BF16)


def _even_in_body(x_ref, sh_ref, sc_ref, ng_ref, w_ref, vng_ref, ws_ref, bsb_ref, gq_ref, gk_ref,
                  cos_ref, sin_ref, bd_ref, *rest, full, tm, q_scale):
    if full:
        ya_ref, q_ref, k_ref, v_ref, sbg_ref, h_s, au_s, gv_s, vn_s = rest
    else:
        k_ref, v_ref, h_s = rest
    sec0 = 0 if full else 4
    col_chunks = range(0, SEC_W, MXU_N)

    h_s[...] = _norm_modulate(x_ref[...], ng_ref[...], sh_ref[...], sc_ref[...]).astype(BF16)

    def proj(sec, c0):
        w0 = (sec - sec0) * SEC_W + c0
        return jnp.dot(h_s[...], w_ref[:, w0:w0 + MXU_N], preferred_element_type=F32)

    def proj_section(sec):
        w0 = (sec - sec0) * SEC_W
        return jnp.dot(h_s[...], w_ref[:, w0:w0 + SEC_W], preferred_element_type=F32)

    if full:
        for c0 in col_chunks:
            au_s[:, c0:c0 + MXU_N] = _gelu(proj(0, c0)).astype(BF16)

        ssq = jnp.zeros((tm, LANES), F32)
        for c0 in col_chunks:
            gv = _gelu(proj(1, c0))
            gv_s[:, c0:c0 + MXU_N] = gv
            for l in range(0, MXU_N, LANES):
                ssq = ssq + gv[:, l:l + LANES] * gv[:, l:l + LANES]
        inv = lax.rsqrt(jnp.sum(ssq, axis=-1, keepdims=True) * (1.0 / A_WIDTH) + EPS)
        vn_s[...] = (gv_s[...] * inv * vng_ref[...]).astype(BF16)

        for c0 in col_chunks:
            sg = _silu(proj(2, c0))
            for r0 in range(0, tm, CHUNK):
                rows = slice(r0, r0 + CHUNK)
                for l in range(0, MXU_N, LANES):
                    g = (c0 + l) // LANES
                    cols = slice(c0 + l, c0 + l + LANES)
                    mixed = jnp.dot(ws_ref[g], vn_s[rows, cols], preferred_element_type=F32) + bsb_ref[g]
                    ya_ref[rows, cols] = (au_s[rows, cols].astype(F32) * mixed * sg[rows, l:l + LANES]).astype(BF16)

        _qk_norm_rope(proj_section(3), 0, gq_ref, bd_ref, cos_ref, sin_ref, q_ref, True, q_scale)

        for c0 in col_chunks:
            sbg_ref[:, c0:c0 + MXU_N] = _silu(proj(6, c0)).astype(BF16)

    _qk_norm_rope(proj_section(4), 0, gk_ref, bd_ref, cos_ref, sin_ref, k_ref, full, 1.0)

    for c0 in col_chunks:
        v_ref[:, c0:c0 + MXU_N] = proj(5, c0).astype(BF16)


def _even_in(x2d, sh, sc, ng, w_bf, vng, ws_bf, bsb, gq, gk, cos, sin, bd, *, full, tm, rows_per_batch,
             q_scale):
    m = x2d.shape[0]
    tiles_per_batch = rows_per_batch // tm
    row = lambda i: (i, 0)
    const2 = lambda i: (0, 0)
    const3 = lambda i: (0, 0, 0)
    batch3 = lambda i: (i // tiles_per_batch, 0, 0)
    pos = lambda i: (i % tiles_per_batch, 0)
    if full:
        w_spec = _resident(w_bf.shape)
    else:
        w_spec = pl.BlockSpec((D_MODEL, 2 * SEC_W), lambda i: (0, 2), pipeline_mode=pl.Buffered(1))
    in_specs = [
        pl.BlockSpec((tm, D_MODEL), row),
        pl.BlockSpec((None, 1, D_MODEL), batch3),
        pl.BlockSpec((None, 1, D_MODEL), batch3),
        pl.BlockSpec((1, D_MODEL), const2),
        w_spec,
        pl.BlockSpec((1, A_WIDTH), const2),
        pl.BlockSpec((A_GROUPS, CHUNK, CHUNK), const3),
        pl.BlockSpec((A_GROUPS, CHUNK, LANES), const3),
        pl.BlockSpec((1, HEAD_W), const2),
        pl.BlockSpec((1, HEAD_W), const2),
        pl.BlockSpec((tm, HEAD_W), pos),
        pl.BlockSpec((tm, HEAD_W), pos),
        pl.BlockSpec((HEAD_W, HEAD_W), const2),
    ]
    sec_out = jax.ShapeDtypeStruct((m, SEC_W), BF16)
    n_out = 5 if full else 2
    scratch = [pltpu.VMEM((tm, D_MODEL), BF16)]
    if full:
        scratch += [pltpu.VMEM((tm, A_WIDTH), BF16), pltpu.VMEM((tm, A_WIDTH), F32),
                    pltpu.VMEM((tm, A_WIDTH), BF16)]
    return pl.pallas_call(
        functools.partial(_even_in_body, full=full, tm=tm, q_scale=q_scale),
        grid=(m // tm,),
        in_specs=in_specs,
        out_specs=[pl.BlockSpec((tm, SEC_W), row)] * n_out,
        out_shape=[sec_out] * n_out,
        scratch_shapes=scratch,
        compiler_params=_cparams(("parallel",)),
        name="even_in" if full else "even_in_ctx",
    )(x2d, sh, sc, ng, w_bf, vng, ws_bf, bsb, gq, gk, cos, sin, bd)


def _attn_body(q_ref, k_ref, kc_ref, v_ref, vc_ref, sbg_ref, lam_ref, og_ref, o_ref,
               kk_s, v1t_s, s_s, *, tq, tg, tk, n_lat, n_ctx, lambda_init):
    i = pl.program_id(2)
    n_tot = n_lat + n_ctx
    chunks = [(c0, min(tk, n_tot - c0)) for c0 in range(0, n_tot, tk)]

    @pl.when(i == 0)
    def _():
        kk_s[0:n_lat, :] = k_ref[...]
        kk_s[n_lat:n_tot, :] = kc_ref[...]
        for c0, w in chunks:
            src, r0 = (v_ref, c0) if c0 < n_lat else (vc_ref, c0 - n_lat)
            v1t_s[0:HEAD_W, c0:c0 + w] = src[r0:r0 + w, :].astype(F32).T.astype(BF16)
        v1t_s[HEAD_W:HEAD_W + BF16_ROWS, :] = jnp.ones((BF16_ROWS, n_tot), BF16)

    qt = q_ref[...].astype(F32).T
    row = lax.broadcasted_iota(jnp.int32, qt.shape, 0)
    qts = (jnp.where(row < B_QK_DIM, qt, 0.0).astype(BF16), jnp.where(row >= B_QK_DIM, qt, 0.0).astype(BF16))
    groups = [(t, r0) for r0 in range(0, tq, tg) for t in range(2)]
    slot0 = lambda g: (g % S_SLOTS) * n_tot

    def scores(g):
        t, r0 = groups[g]
        qg = qts[t][:, r0:r0 + tg]
        mx = jnp.full((1, tg), -jnp.inf, F32)
        for c0, w in chunks:
            sc = jnp.dot(kk_s[c0:c0 + w, :], qg, preferred_element_type=F32)
            s_s[slot0(g) + c0:slot0(g) + c0 + w, :] = sc
            mx = jnp.maximum(mx, jnp.max(sc, axis=0, keepdims=True))
        return mx

    def weighted_sum(g, mx):
        acc = None
        for c0, w in chunks:
            p = jnp.exp2(s_s[slot0(g) + c0:slot0(g) + c0 + w, :] - mx).astype(BF16)
            d = jnp.dot(v1t_s[:, c0:c0 + w], p, preferred_element_type=F32)
            acc = d if acc is None else acc + d
        return acc

    n_groups = len(groups)
    accs = [None] * n_groups
    mx_prev = scores(0)
    for g in range(1, n_groups):
        mx_next = scores(g)
        accs[g - 1] = weighted_sum(g - 1, mx_prev)
        mx_prev = mx_next
    accs[n_groups - 1] = weighted_sum(n_groups - 1, mx_prev)

    lv = lam_ref[...]
    lam = (jnp.exp(jnp.sum(lv[0:1, :] * lv[1:2, :], axis=-1, keepdims=True))
           - jnp.exp(jnp.sum(lv[2:3, :] * lv[3:4, :], axis=-1, keepdims=True)) + lambda_init)
    for n, r0 in enumerate(range(0, tq, tg)):
        a1, a2 = accs[2 * n], accs[2 * n + 1]
        ot = (a1[0:HEAD_W, :] / a1[HEAD_W:HEAD_W + 1, :]
              - lam * (a2[0:HEAD_W, :] / a2[HEAD_W:HEAD_W + 1, :]))
        ms = jnp.mean(ot * ot, axis=0, keepdims=True)
        o = (ot * lax.rsqrt(ms + EPS)).T
        y = (o * og_ref[...]) * (1.0 - lambda_init)
        o_ref[r0:r0 + tg, :] = (y * sbg_ref[r0:r0 + tg, :].astype(F32)).astype(BF16)


def _attention(q, k, kc, v, vc, sbg, lam_vecs, og, *, tq, tg, tk, lambda_init):
    bsz, n_lat, _ = q.shape
    n_ctx = kc.shape[1]
    n_tot = n_lat + n_ctx
    qtile = pl.BlockSpec((None, tq, HEAD_W), lambda b, h, i: (b, i, h))
    whole = lambda n: pl.BlockSpec((None, n, HEAD_W), lambda b, h, i: (b, 0, h))
    const2 = lambda b, h, i: (0, 0)
    return pl.pallas_call(
        functools.partial(_attn_body, tq=tq, tg=tg, tk=tk, n_lat=n_lat, n_ctx=n_ctx, lambda_init=lambda_init),
        grid=(bsz, B_HEADS, n_lat // tq),
        in_specs=[qtile, whole(n_lat), whole(n_ctx), whole(n_lat), whole(n_ctx), qtile,
                  pl.BlockSpec((4, B_QK_DIM), const2), pl.BlockSpec((1, HEAD_W), const2)],
        out_specs=qtile,
        out_shape=jax.ShapeDtypeStruct((bsz, n_lat, B_WIDTH), BF16),
        scratch_shapes=[pltpu.VMEM((n_tot, HEAD_W), BF16),
                        pltpu.VMEM((HEAD_W + BF16_ROWS, n_tot), BF16),
                        pltpu.VMEM((S_SLOTS * n_tot, tg), F32)],
        compiler_params=_cparams(("parallel", "parallel", "arbitrary")),
        name="diff_attn",
    )(q, k, kc, v, vc, sbg, lam_vecs, og)


def _even_out_body(ya_ref, yb_ref, wa_ref, wb_ref, x_ref, gt_ref, o_ref):
    y = (jnp.dot(ya_ref[...], wa_ref[...], preferred_element_type=F32)
         + jnp.dot(yb_ref[...], wb_ref[...], preferred_element_type=F32))
    o_ref[...] = x_ref[...] + gt_ref[...] * y


def _even_out(ya, yb, w_bf, x2d, gt, *, tm, rows_per_batch):
    m = x2d.shape[0]
    tiles_per_batch = rows_per_batch // tm
    row = lambda i: (i, 0)
    return pl.pallas_call(
        _even_out_body,
        grid=(m // tm,),
        in_specs=[pl.BlockSpec((tm, A_WIDTH), row), pl.BlockSpec((tm, B_WIDTH), row),
                  pl.BlockSpec((A_WIDTH, D_MODEL), lambda i: (0, 0)),
                  pl.BlockSpec((B_WIDTH, D_MODEL), lambda i: (1, 0)),
                  pl.BlockSpec((tm, D_MODEL), row),
                  pl.BlockSpec((None, 1, D_MODEL), lambda i: (i // tiles_per_batch, 0, 0))],
        out_specs=pl.BlockSpec((tm, D_MODEL), row),
        out_shape=jax.ShapeDtypeStruct((m, D_MODEL), F32),
        compiler_params=_cparams(("parallel",)),
        name="even_out",
    )(ya, yb, w_bf, w_bf, x2d, gt)


def _odd_in_body(x_ref, sh_ref, sc_ref, ng_ref, w_ref, y_ref, sg_ref, h_s):
    h_s[...] = _norm_modulate(x_ref[...], ng_ref[...], sh_ref[...], sc_ref[...]).astype(BF16)

    def proj(c0):
        return jnp.dot(h_s[...], w_ref[:, c0:c0 + MXU_N], preferred_element_type=F32)

    for c0 in range(0, D_MODEL, MXU_N):
        y_ref[:, c0:c0 + MXU_N] = (proj(c0) * jax.nn.sigmoid(proj(D_MODEL + c0))).astype(BF16)
        sg_ref[:, c0:c0 + MXU_N] = _silu(proj(2 * D_MODEL + c0)).astype(BF16)


def _odd_in(x2d, sh, sc, ng, w_bf, *, tm, rows_per_batch):
    m = x2d.shape[0]
    tiles_per_batch = rows_per_batch // tm
    row = lambda i: (i, 0)
    batch3 = lambda i: (i // tiles_per_batch, 0, 0)
    out = jax.ShapeDtypeStruct((m, D_MODEL), BF16)
    return pl.pallas_call(
        _odd_in_body,
        grid=(m // tm,),
        in_specs=[pl.BlockSpec((tm, D_MODEL), row),
                  pl.BlockSpec((None, 1, D_MODEL), batch3),
                  pl.BlockSpec((None, 1, D_MODEL), batch3),
                  pl.BlockSpec((1, D_MODEL), lambda i: (0, 0)),
                  _resident(w_bf.shape)],
        out_specs=[pl.BlockSpec((tm, D_MODEL), row)] * 2,
        out_shape=[out, out],
        scratch_shapes=[pltpu.VMEM((tm, D_MODEL), BF16)],
        compiler_params=_cparams(("parallel",)),
        name="odd_in",
    )(x2d, sh, sc, ng, w_bf)


def _odd_out_body(y_ref, yp_ref, yn_ref, sg_ref, x_ref, gt_ref, dw_ref, db_ref, lg_ref, lb_ref, w_ref,
                  o_ref, ybuf_s, conv_s, u_s, *, tm, tiles_per_batch):
    i = pl.program_id(0)
    t = i % tiles_per_batch
    n_slabs = D_MODEL // LANES
    for cb in range(n_slabs):
        cols = slice(cb * LANES, (cb + 1) * LANES)
        ybuf_s[cb, 0:HALO, :] = jnp.where(t > 0, yp_ref[:, cols].astype(F32), 0.0)
        ybuf_s[cb, HALO:HALO + tm, :] = y_ref[:, cols].astype(F32)
        ybuf_s[cb, HALO + tm:2 * HALO + tm, :] = jnp.where(t < tiles_per_batch - 1, yn_ref[:, cols].astype(F32), 0.0)

    span = SUBLANES * ROW_STRIDE
    bases = [blk * span + s for blk in range(tm // span) for s in range(ROW_STRIDE)]
    group = 8

    def col_block(cb, carry):
        c0 = pl.multiple_of(cb * LANES, LANES)
        bias = jnp.broadcast_to(db_ref[:, pl.ds(c0, LANES)], (SUBLANES, LANES))
        for g0 in range(0, len(bases), group):
            accs = [bias] * group
            for k in range(CONV_W):
                wk = dw_ref[k, :, pl.ds(c0, LANES)]
                for n, b in enumerate(bases[g0:g0 + group]):
                    rows = pl.ds(b + k + HALO - CONV_PAD, SUBLANES, stride=ROW_STRIDE)
                    accs[n] = accs[n] + ybuf_s[cb, rows, :] * wk
            for n, b in enumerate(bases[g0:g0 + group]):
                conv_s[cb, pl.ds(b, SUBLANES, stride=ROW_STRIDE), :] = accs[n]
        return carry

    lax.fori_loop(0, n_slabs, col_block, 0)

    tot = conv_s[0]
    for cb in range(1, n_slabs):
        tot = tot + conv_s[cb]
    mu = jnp.sum(tot, axis=-1, keepdims=True) * (1.0 / D_MODEL)
    sq = jnp.zeros((tm, LANES), F32)
    for cb in range(n_slabs):
        d = conv_s[cb] - mu
        sq = sq + d * d
    inv = lax.rsqrt(jnp.sum(sq, axis=-1, keepdims=True) * (1.0 / D_MODEL) + EPS)
    for cb in range(n_slabs):
        cols = slice(cb * LANES, (cb + 1) * LANES)
        z = (conv_s[cb] - mu) * inv * lg_ref[:, cols] + lb_ref[:, cols]
        u_s[:, cols] = (_silu(z) * sg_ref[:, cols].astype(F32)).astype(BF16)
    o_ref[...] = x_ref[...] + gt_ref[...] * jnp.dot(u_s[...], w_ref[...], preferred_element_type=F32)


def _odd_out(y, sg, x2d, gt, dw8, dw_b, ln_g, ln_b, w_bf, *, tm, rows_per_batch):
    m = x2d.shape[0]
    tiles_per_batch = rows_per_batch // tm
    hpt = tm // HALO
    last_halo = m // HALO - 1
    row = lambda i: (i, 0)
    const2 = lambda i: (0, 0)
    return pl.pallas_call(
        functools.partial(_odd_out_body, tm=tm, tiles_per_batch=tiles_per_batch),
        grid=(m // tm,),
        in_specs=[pl.BlockSpec((tm, D_MODEL), row),
                  pl.BlockSpec((HALO, D_MODEL), lambda i: (jnp.maximum(i * hpt - 1, 0), 0)),
                  pl.BlockSpec((HALO, D_MODEL), lambda i: (jnp.minimum((i + 1) * hpt, last_halo), 0)),
                  pl.BlockSpec((tm, D_MODEL), row),
                  pl.BlockSpec((tm, D_MODEL), row),
                  pl.BlockSpec((None, 1, D_MODEL), lambda i: (i // tiles_per_batch, 0, 0)),
                  _resident(dw8.shape),
                  pl.BlockSpec((1, D_MODEL), const2),
                  pl.BlockSpec((1, D_MODEL), const2),
                  pl.BlockSpec((1, D_MODEL), const2),
                  _resident(w_bf.shape)],
        out_specs=pl.BlockSpec((tm, D_MODEL), row),
        out_shape=jax.ShapeDtypeStruct((m, D_MODEL), F32),
        scratch_shapes=[pltpu.VMEM((D_MODEL // LANES, tm + 2 * HALO, LANES), F32),
                        pltpu.VMEM((D_MODEL // LANES, tm, LANES), F32),
                        pltpu.VMEM((tm, D_MODEL), BF16)],
        compiler_params=_cparams(("parallel",)),
        name="odd_out",
    )(y, y, y, sg, x2d, gt, dw8, dw_b, ln_g, ln_b, w_bf)


def _rope_tables(n_lat):
    axis_dim = B_QK_DIM // 2
    inv = ROPE_BASE ** (-jnp.arange(0, axis_dim, 2, dtype=F32) / axis_dim)
    rows = n_lat // GRID_W
    r = jnp.repeat(jnp.arange(rows, dtype=F32), GRID_W)
    c = jnp.tile(jnp.arange(GRID_W, dtype=F32), rows)
    ar = r[:, None] * inv[None, :]
    ac = c[:, None] * inv[None, :]
    ang = jnp.concatenate([ar, ar, ac, ac], axis=-1)
    sign = jnp.where((jnp.arange(B_QK_DIM) % 32) < 16, -1.0, 1.0).astype(F32)
    cos = jnp.tile(jnp.cos(ang), (1, 2))
    sin = jnp.tile(jnp.sin(ang) * sign[None, :], (1, 2))
    return cos, sin


def kernel(x, c, ctx, c_ctx, e_norm_g, e_ada_w, e_ada_b, e_w_in, e_a_vnorm_g, e_a_ws, e_a_bs, e_b_qnorm_g, e_b_knorm_g, e_b_lambda, e_b_onorm_g, e_w_out, o_norm_g, o_ada_w, o_ada_b, o_w_in, o_dw_w, o_dw_b, o_ln_g, o_ln_b, o_w_out):
    bsz, n_lat, d = x.shape
    n_ctx = ctx.shape[1]
    x2d = x.reshape(bsz * n_lat, d)
    ctx2d = ctx.reshape(bsz * n_ctx, d)
    cond = jnp.concatenate([c, c_ctx[None, :], jnp.zeros((8 - bsz - 1, d), F32)], axis=0)
    row3 = lambda t: t.reshape(t.shape[0], 1, d)

    lambda_init = 0.8 - 0.6 * math.exp(-0.3 * 0)
    mod = _adaln(cond, e_ada_w[0], e_ada_b[0])
    sh, sc, gt = (row3(mod[:bsz, n * d:(n + 1) * d]) for n in range(3))
    csh, csc = (jnp.broadcast_to(mod[bsz:bsz + 1, n * d:(n + 1) * d], (bsz, d)).reshape(bsz, 1, d)
                for n in range(2))
    cos, sin = _rope_tables(n_lat)
    gid = jnp.arange(HEAD_W) // B_QK_DIM
    bd = jnp.where(gid[:, None] == gid[None, :], 1.0 / B_QK_DIM, 0.0).astype(BF16)
    tile2 = lambda g: jnp.tile(g, 2).reshape(1, HEAD_W)
    w_in_bf = e_w_in[0].astype(BF16)
    common = (e_norm_g[0].reshape(1, d), w_in_bf, e_a_vnorm_g[0].reshape(1, A_WIDTH),
              e_a_ws[0].astype(BF16),
              jnp.broadcast_to(e_a_bs[0][:, :, None], (A_GROUPS, CHUNK, LANES)),
              tile2(e_b_qnorm_g[0]), tile2(e_b_knorm_g[0]), cos, sin, bd)
    q_scale = (B_QK_DIM ** -0.5) * LOG2E
    ya, q, k, v, sbg = _even_in(x2d, sh, sc, *common, full=True, tm=256, rows_per_batch=n_lat,
                                q_scale=q_scale)
    kc, vc = _even_in(ctx2d, csh, csc, *common, full=False, tm=n_ctx, rows_per_batch=n_ctx,
                      q_scale=q_scale)
    seq = lambda t, n: t.reshape(bsz, n, B_WIDTH)
    yb = _attention(seq(q, n_lat), seq(k, n_lat), seq(kc, n_ctx), seq(v, n_lat), seq(vc, n_ctx),
                    seq(sbg, n_lat), e_b_lambda[0], e_b_onorm_g[0].reshape(1, HEAD_W),
                    tq=2048, tg=256, tk=512, lambda_init=lambda_init)
    x1 = _even_out(ya, yb.reshape(bsz * n_lat, B_WIDTH), e_w_out[0].astype(BF16), x2d, gt,
                   tm=512, rows_per_batch=n_lat)

    mod = _adaln(cond, o_ada_w[0], o_ada_b[0])
    sh, sc, gt = (row3(mod[:bsz, n * d:(n + 1) * d]) for n in range(3))
    y, sg = _odd_in(x1, sh, sc, o_norm_g[0].reshape(1, d), o_w_in[0].astype(BF16),
                    tm=256, rows_per_batch=n_lat)
    dw8 = jnp.broadcast_to(o_dw_w[0][:, None, :], (CONV_W, SUBLANES, d))
    x2 = _odd_out(y, sg, x1, gt, dw8, o_dw_b[0].reshape(1, d), o_ln_g[0].reshape(1, d),
                  o_ln_b[0].reshape(1, d), o_w_out[0].astype(BF16), tm=256, rows_per_batch=n_lat)
    return x2.reshape(bsz, n_lat, d)
```

```python
import functools
import math

import jax
import jax.numpy as jnp
from jax import lax
from jax.experimental import pallas as pl
from jax.experimental.pallas import tpu as pltpu

F32 = jnp.float32
BF16 = jnp.bfloat16

D_MODEL = 2048
GRID_W = 64
EPS = 1e-6
CHUNK = 128
A_WIDTH = 1024
A_GROUPS = 8
B_HEADS = 8
B_QK_DIM = 64
HEAD_W = 128
B_WIDTH = 1024
ROPE_BASE = 10000.0
CONV_W = 31
CONV_PAD = 15
SEC_W = 1024
LANES = 128
SUBLANES = 8
BF16_ROWS = 16
MXU_N = 256
HALO = 16
ROW_STRIDE = 4
S_SLOTS = 3
VMEM_LIMIT = 56 * 1024 * 1024
LOG2E = 1.4426950408889634


def _cparams(sem):
    return pltpu.CompilerParams(dimension_semantics=sem, vmem_limit_bytes=VMEM_LIMIT)


def _resident(shape):
    return pl.BlockSpec(shape, lambda *_: (0,) * len(shape), pipeline_mode=pl.Buffered(1))


def _silu(x):
    return x * jax.nn.sigmoid(x)


def _gelu(x):
    return 0.5 * x * (1.0 + lax.erf(x * (2.0 ** -0.5)))


def _adaln_body(c_ref, w_ref, b_ref, o_ref):
    s = _silu(c_ref[...]).astype(BF16)
    o_ref[...] = jnp.dot(s, w_ref[...].astype(BF16), preferred_element_type=F32) + b_ref[...]


def _adaln(cond, w, b):
    tn = 512
    n = w.shape[1]
    return pl.pallas_call(
        _adaln_body,
        grid=(n // tn,),
        in_specs=[pl.BlockSpec((8, D_MODEL), lambda j: (0, 0)),
                  pl.BlockSpec((D_MODEL, tn), lambda j: (0, j)),
                  pl.BlockSpec((1, tn), lambda j: (0, j))],
        out_specs=pl.BlockSpec((8, tn), lambda j: (0, j)),
        out_shape=jax.ShapeDtypeStruct((8, n), F32),
        compiler_params=_cparams(("parallel",)),
        name="adaln",
    )(cond, w, b.reshape(1, n))


def _norm_modulate(x, ng, sh, sc):
    ms = jnp.mean(x * x, axis=-1, keepdims=True)
    return (x * lax.rsqrt(ms + EPS) * ng) * (1.0 + sc) + sh


def _qk_norm_rope(acc, col0, g_ref, bd_ref, cos_ref, sin_ref, out_ref, rope, scale):
    for hb in range(acc.shape[1] // HEAD_W):
        t = acc[:, hb * HEAD_W:(hb + 1) * HEAD_W]
        ms = jnp.dot((t * t).astype(BF16), bd_ref[...], preferred_element_type=F32)
        t = t * lax.rsqrt(ms + EPS) * g_ref[...]
        if rope:
            lane = lax.broadcasted_iota(jnp.int32, t.shape, 1)
            r = jnp.where((lane & 31) < 16, pltpu.roll(t, LANES - 16, 1), pltpu.roll(t, 16, 1))
            t = t * cos_ref[...] + r * sin_ref[...]
        if scale != 1.0:
            t = t * scale
        out_ref[:, col0 + hb * HEAD_W:col0 + (hb + 1) * HEAD_W] = t.astype(BF16)


def _even_in_body(x_ref, sh_ref, sc_ref, ng_ref, w_ref, vng_ref, ws_ref, bsb_ref, gq_ref, gk_ref,
                  cos_ref, sin_ref, bd_ref, *rest, full, tm, q_scale):
    if full:
        ya_ref, q_ref, k_ref, v_ref, sbg_ref, h_s, au_s, gv_s, vn_s = rest
    else:
        k_ref, v_ref, h_s = rest
    sec0 = 0 if full else 4
    col_chunks = range(0, SEC_W, MXU_N)

    h_s[...] = _norm_modulate(x_ref[...], ng_ref[...], sh_ref[...], sc_ref[...]).astype(BF16)

    def proj(sec, c0):
        w0 = (sec - sec0) * SEC_W + c0
        return jnp.dot(h_s[...], w_ref[:, w0:w0 + MXU_N], preferred_element_type=F32)

    def proj_section(sec):
        w0 = (sec - sec0) * SEC_W
        return jnp.dot(h_s[...], w_ref[:, w0:w0 + SEC_W], preferred_element_type=F32)

    if full:
        for c0 in col_chunks:
            au_s[:, c0:c0 + MXU_N] = _gelu(proj(0, c0)).astype(BF16)

        ssq = jnp.zeros((tm, LANES), F32)
        for c0 in col_chunks:
            gv = _gelu(proj(1, c0))
            gv_s[:, c0:c0 + MXU_N] = gv
            for l in range(0, MXU_N, LANES):
                ssq = ssq + gv[:, l:l + LANES] * gv[:, l:l + LANES]
        inv = lax.rsqrt(jnp.sum(ssq, axis=-1, keepdims=True) * (1.0 / A_WIDTH) + EPS)
        vn_s[...] = (gv_s[...] * inv * vng_ref[...]).astype(BF16)

        for c0 in col_chunks:
            sg = _silu(proj(2, c0))
            for r0 in range(0, tm, CHUNK):
                rows = slice(r0, r0 + CHUNK)
                for l in range(0, MXU_N, LANES):
                    g = (c0 + l) // LANES
                    cols = slice(c0 + l, c0 + l + LANES)
                    mixed = jnp.dot(ws_ref[g], vn_s[rows, cols], preferred_element_type=F32) + bsb_ref[g]
                    ya_ref[rows, cols] = (au_s[rows, cols].astype(F32) * mixed * sg[rows, l:l + LANES]).astype(BF16)

        _qk_norm_rope(proj_section(3), 0, gq_ref, bd_ref, cos_ref, sin_ref, q_ref, True, q_scale)

        for c0 in col_chunks:
            sbg_ref[:, c0:c0 + MXU_N] = _silu(proj(6, c0)).astype(BF16)

    _qk_norm_rope(proj_section(4), 0, gk_ref, bd_ref, cos_ref, sin_ref, k_ref, full, 1.0)

    for c0 in col_chunks:
        v_ref[:, c0:c0 + MXU_N] = proj(5, c0).astype(BF16)


def _even_in(x2d, sh, sc, ng, w_bf, vng, ws_bf, bsb, gq, gk, cos, sin, bd, *, full, tm, rows_per_batch,
             q_scale):
    m = x2d.shape[0]
    tiles_per_batch = rows_per_batch // tm
    row = lambda i: (i, 0)
    const2 = lambda i: (0, 0)
    const3 = lambda i: (0, 0, 0)
    batch3 = lambda i: (i // tiles_per_batch, 0, 0)
    pos = lambda i: (i % tiles_per_batch, 0)
    if full:
        w_spec = _resident(w_bf.shape)
    else:
        w_spec = pl.BlockSpec((D_MODEL, 2 * SEC_W), lambda i: (0, 2), pipeline_mode=pl.Buffered(1))
    in_specs = [
        pl.BlockSpec((tm, D_MODEL), row),
        pl.BlockSpec((None, 1, D_MODEL), batch3),
        pl.BlockSpec((None, 1, D_MODEL), batch3),
        pl.BlockSpec((1, D_MODEL), const2),
        w_spec,
        pl.BlockSpec((1, A_WIDTH), const2),
        pl.BlockSpec((A_GROUPS, CHUNK, CHUNK), const3),
        pl.BlockSpec((A_GROUPS, CHUNK, LANES), const3),
        pl.BlockSpec((1, HEAD_W), const2),
        pl.BlockSpec((1, HEAD_W), const2),
        pl.BlockSpec((tm, HEAD_W), pos),
        pl.BlockSpec((tm, HEAD_W), pos),
        pl.BlockSpec((HEAD_W, HEAD_W), const2),
    ]
    sec_out = jax.ShapeDtypeStruct((m, SEC_W), BF16)
    n_out = 5 if full else 2
    scratch = [pltpu.VMEM((tm, D_MODEL), BF16)]
    if full:
        scratch += [pltpu.VMEM((tm, A_WIDTH), BF16), pltpu.VMEM((tm, A_WIDTH), F32),
                    pltpu.VMEM((tm, A_WIDTH), BF16)]
    return pl.pallas_call(
        functools.partial(_even_in_body, full=full, tm=tm, q_scale=q_scale),
        grid=(m // tm,),
        in_specs=in_specs,
        out_specs=[pl.BlockSpec((tm, SEC_W), row)] * n_out,
        out_shape=[sec_out] * n_out,
        scratch_shapes=scratch,
        compiler_params=_cparams(("parallel",)),
        name="even_in" if full else "even_in_ctx",
    )(x2d, sh, sc, ng, w_bf, vng, ws_bf, bsb, gq, gk, cos, sin, bd)


def _attn_body(q_ref, k_ref, kc_ref, v_ref, vc_ref, sbg_ref, lam_ref, og_ref, o_ref,
               kk_s, v1t_s, s_s, *, tq, tg, tk, n_lat, n_ctx, lambda_init):
    i = pl.program_id(2)
    n_tot = n_lat + n_ctx
    chunks = [(c0, min(tk, n_tot - c0)) for c0 in range(0, n_tot, tk)]

    @pl.when(i == 0)
    def _():
        kk_s[0:n_lat, :] = k_ref[...]
        kk_s[n_lat:n_tot, :] = kc_ref[...]
        for c0, w in chunks:
            src, r0 = (v_ref, c0) if c0 < n_lat else (vc_ref, c0 - n_lat)
            v1t_s[0:HEAD_W, c0:c0 + w] = src[r0:r0 + w, :].astype(F32).T.astype(BF16)
        v1t_s[HEAD_W:HEAD_W + BF16_ROWS, :] = jnp.ones((BF16_ROWS, n_tot), BF16)

    qt = q_ref[...].astype(F32).T
    row = lax.broadcasted_iota(jnp.int32, qt.shape, 0)
    qts = (jnp.where(row < B_QK_DIM, qt, 0.0).astype(BF16), jnp.where(row >= B_QK_DIM, qt, 0.0).astype(BF16))
    groups = [(t, r0) for r0 in range(0, tq, tg) for t in range(2)]
    slot0 = lambda g: (g % S_SLOTS) * n_tot

    def scores(g):
        t, r0 = groups[g]
        qg = qts[t][:, r0:r0 + tg]
        mx = jnp.full((1, tg), -jnp.inf, F32)
        for c0, w in chunks:
            sc = jnp.dot(kk_s[c0:c0 + w, :], qg, preferred_element_type=F32)
            s_s[slot0(g) + c0:slot0(g) + c0 + w, :] = sc
            mx = jnp.maximum(mx, jnp.max(sc, axis=0, keepdims=True))
        return mx

    def weighted_sum(g, mx):
        acc = None
        for c0, w in chunks:
            p = jnp.exp2(s_s[slot0(g) + c0:slot0(g) + c0 + w, :] - mx).astype(BF16)
            d = jnp.dot(v1t_s[:, c0:c0 + w], p, preferred_element_type=F32)
            acc = d if acc is None else acc + d
        return acc

    n_groups = len(groups)
    accs = [None] * n_groups
    mx_prev = scores(0)
    for g in range(1, n_groups):
        mx_next = scores(g)
        accs[g - 1] = weighted_sum(g - 1, mx_prev)
        mx_prev = mx_next
    accs[n_groups - 1] = weighted_sum(n_groups - 1, mx_prev)

    lv = lam_ref[...]
    lam = (jnp.exp(jnp.sum(lv[0:1, :] * lv[1:2, :], axis=-1, keepdims=True))
           - jnp.exp(jnp.sum(lv[2:3, :] * lv[3:4, :], axis=-1, keepdims=True)) + lambda_init)
    for n, r0 in enumerate(range(0, tq, tg)):
        a1, a2 = accs[2 * n], accs[2 * n + 1]
        ot = (a1[0:HEAD_W, :] / a1[HEAD_W:HEAD_W + 1, :]
              - lam * (a2[0:HEAD_W, :] / a2[HEAD_W:HEAD_W + 1, :]))
        ms = jnp.mean(ot * ot, axis=0, keepdims=True)
        o = (ot * lax.rsqrt(ms + EPS)).T
        y = (o * og_ref[...]) * (1.0 - lambda_init)
        o_ref[r0:r0 + tg, :] = (y * sbg_ref[r0:r0 + tg, :].astype(F32)).astype(BF16)


def _attention(q, k, kc, v, vc, sbg, lam_vecs, og, *, tq, tg, tk, lambda_init):
    bsz, n_lat, _ = q.shape
    n_ctx = kc.shape[1]
    n_tot = n_lat + n_ctx
    qtile = pl.BlockSpec((None, tq, HEAD_W), lambda b, h, i: (b, i, h))
    whole = lambda n: pl.BlockSpec((None, n, HEAD_W), lambda b, h, i: (b, 0, h))
    const2 = lambda b, h, i: (0, 0)
    return pl.pallas_call(
        functools.partial(_attn_body, tq=tq, tg=tg, tk=tk, n_lat=n_lat, n_ctx=n_ctx, lambda_init=lambda_init),
        grid=(bsz, B_HEADS, n_lat // tq),
        in_specs=[qtile, whole(n_lat), whole(n_ctx), whole(n_lat), whole(n_ctx), qtile,
                  pl.BlockSpec((4, B_QK_DIM), const2), pl.BlockSpec((1, HEAD_W), const2)],
        out_specs=qtile,
        out_shape=jax.ShapeDtypeStruct((bsz, n_lat, B_WIDTH), BF16),
        scratch_shapes=[pltpu.VMEM((n_tot, HEAD_W), BF16),
                        pltpu.VMEM((HEAD_W + BF16_ROWS, n_tot), BF16),
                        pltpu.VMEM((S_SLOTS * n_tot, tg), F32)],
        compiler_params=_cparams(("parallel", "parallel", "arbitrary")),
        name="diff_attn",
    )(q, k, kc, v, vc, sbg, lam_vecs, og)


def _even_out_body(ya_ref, yb_ref, wa_ref, wb_ref, x_ref, gt_ref, o_ref):
    y = (jnp.dot(ya_ref[...], wa_ref[...], preferred_element_type=F32)
         + jnp.dot(yb_ref[...], wb_ref[...], preferred_element_type=F32))
    o_ref[...] = x_ref[...] + gt_ref[...] * y


def _even_out(ya, yb, w_bf, x2d, gt, *, tm, rows_per_batch):
    m = x2d.shape[0]
    tiles_per_batch = rows_per_batch // tm
    row = lambda i: (i, 0)
    return pl.pallas_call(
        _even_out_body,
        grid=(m // tm,),
        in_specs=[pl.BlockSpec((tm, A_WIDTH), row), pl.BlockSpec((tm, B_WIDTH), row),
                  pl.BlockSpec((A_WIDTH, D_MODEL), lambda i: (0, 0)),
                  pl.BlockSpec((B_WIDTH, D_MODEL), lambda i: (1, 0)),
                  pl.BlockSpec((tm, D_MODEL), row),
                  pl.BlockSpec((None, 1, D_MODEL), lambda i: (i // tiles_per_batch, 0, 0))],
        out_specs=pl.BlockSpec((tm, D_MODEL), row),
        out_shape=jax.ShapeDtypeStruct((m, D_MODEL), F32),
        compiler_params=_cparams(("parallel",)),
        name="even_out",
    )(ya, yb, w_bf, w_bf, x2d, gt)


def _odd_in_body(x_ref, sh_ref, sc_ref, ng_ref, w_ref, y_ref, sg_ref, h_s):
    h_s[...] = _norm_modulate(x_ref[...], ng_ref[...], sh_ref[...], sc_ref[...]).astype(BF16)

    def proj(c0):
        return jnp.dot(h_s[...], w_ref[:, c0:c0 + MXU_N], preferred_element_type=F32)

    for c0 in range(0, D_MODEL, MXU_N):
        y_ref[:, c0:c0 + MXU_N] = (proj(c0) * jax.nn.sigmoid(proj(D_MODEL + c0))).astype(BF16)
        sg_ref[:, c0:c0 + MXU_N] = _silu(proj(2 * D_MODEL + c0)).astype(BF16)


def _odd_in(x2d, sh, sc, ng, w_bf, *, tm, rows_per_batch):
    m = x2d.shape[0]
    tiles_per_batch = rows_per_batch // tm
    row = lambda i: (i, 0)
    batch3 = lambda i: (i // tiles_per_batch, 0, 0)
    out = jax.ShapeDtypeStruct((m, D_MODEL), BF16)
    return pl.pallas_call(
        _odd_in_body,
        grid=(m // tm,),
        in_specs=[pl.BlockSpec((tm, D_MODEL), row),
                  pl.BlockSpec((None, 1, D_MODEL), batch3),
                  pl.BlockSpec((None, 1, D_MODEL), batch3),
                  pl.BlockSpec((1, D_MODEL), lambda i: (0, 0)),
                  _resident(w_bf.shape)],
        out_specs=[pl.BlockSpec((tm, D_MODEL), row)] * 2,
        out_shape=[out, out],
        scratch_shapes=[pltpu.VMEM((tm, D_MODEL), BF16)],
        compiler_params=_cparams(("parallel",)),
        name="odd_in",
    )(x2d, sh, sc, ng, w_bf)


def _odd_out_body(y_ref, yp_ref, yn_ref, sg_ref, x_ref, gt_ref, dw_ref, db_ref, lg_ref, lb_ref, w_ref,
                  o_ref, ybuf_s, conv_s, u_s, *, tm, tiles_per_batch):
    i = pl.program_id(0)
    t = i % tiles_per_batch
    n_slabs = D_MODEL // LANES
    for cb in range(n_slabs):
        cols = slice(cb * LANES, (cb + 1) * LANES)
        ybuf_s[cb, 0:HALO, :] = jnp.where(t > 0, yp_ref[:, cols].astype(F32), 0.0)
        ybuf_s[cb, HALO:HALO + tm, :] = y_ref[:, cols].astype(F32)
        ybuf_s[cb, HALO + tm:2 * HALO + tm, :] = jnp.where(t < tiles_per_batch - 1, yn_ref[:, cols].astype(F32), 0.0)

    span = SUBLANES * ROW_STRIDE
    spans_per_group = 4

    def col_block(cb, carry):
        c0 = pl.multiple_of(cb * LANES, LANES)
        bias = jnp.broadcast_to(db_ref[:, pl.ds(c0, LANES)], (SUBLANES, LANES))
        for g0 in range(0, tm // span, spans_per_group):
            accs = [[bias] * ROW_STRIDE for _ in range(spans_per_group)]
            taps = {}
            for j in range(CONV_W + ROW_STRIDE - 1):
                if j < CONV_W:
                    taps[j] = dw_ref[j, :, pl.ds(c0, LANES)]
                for n in range(spans_per_group):
                    start = (g0 + n) * span + j + HALO - CONV_PAD
                    rows = ybuf_s[cb, pl.ds(start, SUBLANES, stride=ROW_STRIDE), :]
                    for s in range(ROW_STRIDE):
                        if 0 <= j - s < CONV_W:
                            accs[n][s] = accs[n][s] + rows * taps[j - s]
            for n in range(spans_per_group):
                for s in range(ROW_STRIDE):
                    conv_s[cb, pl.ds((g0 + n) * span + s, SUBLANES, stride=ROW_STRIDE), :] = accs[n][s]
        return carry

    lax.fori_loop(0, n_slabs, col_block, 0)

    tot = conv_s[0]
    for cb in range(1, n_slabs):
        tot = tot + conv_s[cb]
    mu = jnp.sum(tot, axis=-1, keepdims=True) * (1.0 / D_MODEL)
    sq = jnp.zeros((tm, LANES), F32)
    for cb in range(n_slabs):
        d = conv_s[cb] - mu
        sq = sq + d * d
    inv = lax.rsqrt(jnp.sum(sq, axis=-1, keepdims=True) * (1.0 / D_MODEL) + EPS)
    for cb in range(n_slabs):
        cols = slice(cb * LANES, (cb + 1) * LANES)
        z = (conv_s[cb] - mu) * inv * lg_ref[:, cols] + lb_ref[:, cols]
        u_s[:, cols] = (_silu(z) * sg_ref[:, cols].astype(F32)).astype(BF16)
    o_ref[...] = x_ref[...] + gt_ref[...] * jnp.dot(u_s[...], w_ref[...], preferred_element_type=F32)


def _odd_out(y, sg, x2d, gt, dw8, dw_b, ln_g, ln_b, w_bf, *, tm, rows_per_batch):
    m = x2d.shape[0]
    tiles_per_batch = rows_per_batch // tm
    hpt = tm // HALO
    last_halo = m // HALO - 1
    row = lambda i: (i, 0)
    const2 = lambda i: (0, 0)
    return pl.pallas_call(
        functools.partial(_odd_out_body, tm=tm, tiles_per_batch=tiles_per_batch),
        grid=(m // tm,),
        in_specs=[pl.BlockSpec((tm, D_MODEL), row),
                  pl.BlockSpec((HALO, D_MODEL), lambda i: (jnp.maximum(i * hpt - 1, 0), 0)),
                  pl.BlockSpec((HALO, D_MODEL), lambda i: (jnp.minimum((i + 1) * hpt, last_halo), 0)),
                  pl.BlockSpec((tm, D_MODEL), row),
                  pl.BlockSpec((tm, D_MODEL), row),
                  pl.BlockSpec((None, 1, D_MODEL), lambda i: (i // tiles_per_batch, 0, 0)),
                  _resident(dw8.shape),
                  pl.BlockSpec((1, D_MODEL), const2),
                  pl.BlockSpec((1, D_MODEL), const2),
                  pl.BlockSpec((1, D_MODEL), const2),
                  _resident(w_bf.shape)],
        out_specs=pl.BlockSpec((tm, D_MODEL), row),
        out_shape=jax.ShapeDtypeStruct((m, D_MODEL), F32),
        scratch_shapes=[pltpu.VMEM((D_MODEL // LANES, tm + 2 * HALO, LANES), F32),
                        pltpu.VMEM((D_MODEL // LANES, tm, LANES), F32),
                        pltpu.VMEM((tm, D_MODEL), BF16)],
        compiler_params=_cparams(("parallel",)),
        name="odd_out",
    )(y, y, y, sg, x2d, gt, dw8, dw_b, ln_g, ln_b, w_bf)


def _rope_tables(n_lat):
    axis_dim = B_QK_DIM // 2
    inv = ROPE_BASE ** (-jnp.arange(0, axis_dim, 2, dtype=F32) / axis_dim)
    rows = n_lat // GRID_W
    r = jnp.repeat(jnp.arange(rows, dtype=F32), GRID_W)
    c = jnp.tile(jnp.arange(GRID_W, dtype=F32), rows)
    ar = r[:, None] * inv[None, :]
    ac = c[:, None] * inv[None, :]
    ang = jnp.concatenate([ar, ar, ac, ac], axis=-1)
    sign = jnp.where((jnp.arange(B_QK_DIM) % 32) < 16, -1.0, 1.0).astype(F32)
    cos = jnp.tile(jnp.cos(ang), (1, 2))
    sin = jnp.tile(jnp.sin(ang) * sign[None, :], (1, 2))
    return cos, sin


def kernel(x, c, ctx, c_ctx, e_norm_g, e_ada_w, e_ada_b, e_w_in, e_a_vnorm_g, e_a_ws, e_a_bs, e_b_qnorm_g, e_b_knorm_g, e_b_lambda, e_b_onorm_g, e_w_out, o_norm_g, o_ada_w, o_ada_b, o_w_in, o_dw_w, o_dw_b, o_ln_g, o_ln_b, o_w_out):
    bsz, n_lat, d = x.shape
    n_ctx = ctx.shape[1]
    x2d = x.reshape(bsz * n_lat, d)
    ctx2d = ctx.reshape(bsz * n_ctx, d)
    cond = jnp.concatenate([c, c_ctx[None, :], jnp.zeros((8 - bsz - 1, d), F32)], axis=0)
    row3 = lambda t: t.reshape(t.shape[0], 1, d)

    lambda_init = 0.8 - 0.6 * math.exp(-0.3 * 0)
    mod = _adaln(cond, e_ada_w[0], e_ada_b[0])
    sh, sc, gt = (row3(mod[:bsz, n * d:(n + 1) * d]) for n in range(3))
    csh, csc = (jnp.broadcast_to(mod[bsz:bsz + 1, n * d:(n + 1) * d], (bsz, d)).reshape(bsz, 1, d)
                for n in range(2))
    cos, sin = _rope_tables(n_lat)
    gid = jnp.arange(HEAD_W) // B_QK_DIM
    bd = jnp.where(gid[:, None] == gid[None, :], 1.0 / B_QK_DIM, 0.0).astype(BF16)
    tile2 = lambda g: jnp.tile(g, 2).reshape(1, HEAD_W)
    w_in_bf = e_w_in[0].astype(BF16)
    common = (e_norm_g[0].reshape(1, d), w_in_bf, e_a_vnorm_g[0].reshape(1, A_WIDTH),
              e_a_ws[0].astype(BF16),
              jnp.broadcast_to(e_a_bs[0][:, :, None], (A_GROUPS, CHUNK, LANES)),
              tile2(e_b_qnorm_g[0]), tile2(e_b_knorm_g[0]), cos, sin, bd)
    q_scale = (B_QK_DIM ** -0.5) * LOG2E
    ya, q, k, v, sbg = _even_in(x2d, sh, sc, *common, full=True, tm=256, rows_per_batch=n_lat,
                                q_scale=q_scale)
    kc, vc = _even_in(ctx2d, csh, csc, *common, full=False, tm=n_ctx, rows_per_batch=n_ctx,
                      q_scale=q_scale)
    seq = lambda t, n: t.reshape(bsz, n, B_WIDTH)
    yb = _attention(seq(q, n_lat), seq(k, n_lat), seq(kc, n_ctx), seq(v, n_lat), seq(vc, n_ctx),
                    seq(sbg, n_lat), e_b_lambda[0], e_b_onorm_g[0].reshape(1, HEAD_W),
                    tq=2048, tg=256, tk=512, lambda_init=lambda_init)
    x1 = _even_out(ya, yb.reshape(bsz * n_lat, B_WIDTH), e_w_out[0].astype(BF16), x2d, gt,
                   tm=512, rows_per_batch=n_lat)

    mod = _adaln(cond, o_ada_w[0], o_ada_b[0])
    sh, sc, gt = (row3(mod[:bsz, n * d:(n + 1) * d]) for n in range(3))
    y, sg = _odd_in(x1, sh, sc, o_norm_g[0].reshape(1, d), o_w_in[0].astype(BF16),
                    tm=512, rows_per_batch=n_lat)
    dw8 = jnp.broadcast_to(o_dw_w[0][:, None, :], (CONV_W, SUBLANES, d))
    x2 = _odd_out(y, sg, x1, gt, dw8, o_dw_b[0].reshape(1, d), o_ln_g[0].reshape(1, d),
                  o_ln_b[0].reshape(1, d), o_w_out[0].astype(BF16), tm=256, rows_per_batch=n_lat)
    return x2.reshape(bsz, n_lat, d)
```

```python
import functools
import math

import jax
import jax.numpy as jnp
from jax import lax
from jax.experimental import pallas as pl
from jax.experimental.pallas import tpu as pltpu

F32 = jnp.float32
BF16 = jnp.bfloat16

D_MODEL = 2048
GRID_W = 64
EPS = 1e-6
CHUNK = 128
A_WIDTH = 1024
A_GROUPS = 8
B_HEADS = 8
B_QK_DIM = 64
HEAD_W = 128
B_WIDTH = 1024
ROPE_BASE = 10000.0
CONV_W = 31
CONV_PAD = 15
SEC_W = 1024
LANES = 128
SUBLANES = 8
BF16_ROWS = 16
MXU_N = 256
HALO = 16
ROW_STRIDE = 4
S_SLOTS = 3
VMEM_LIMIT = 56 * 1024 * 1024
LOG2E = 1.4426950408889634


def _cparams(sem):
    return pltpu.CompilerParams(dimension_semantics=sem, vmem_limit_bytes=VMEM_LIMIT)


def _resident(shape):
    return pl.BlockSpec(shape, lambda *_: (0,) * len(shape), pipeline_mode=pl.Buffered(1))


def _silu(x):
    return x * jax.nn.sigmoid(x)


def _gelu(x):
    return 0.5 * x * (1.0 + lax.erf(x * (2.0 ** -0.5)))


def _adaln_body(c_ref, w_ref, b_ref, o_ref):
    s = _silu(c_ref[...]).astype(BF16)
    o_ref[...] = jnp.dot(s, w_ref[...].astype(BF16), preferred_element_type=F32) + b_ref[...]


def _adaln(cond, w, b):
    tn = 512
    n = w.shape[1]
    return pl.pallas_call(
        _adaln_body,
        grid=(n // tn,),
        in_specs=[pl.BlockSpec((8, D_MODEL), lambda j: (0, 0)),
                  pl.BlockSpec((D_MODEL, tn), lambda j: (0, j)),
                  pl.BlockSpec((1, tn), lambda j: (0, j))],
        out_specs=pl.BlockSpec((8, tn), lambda j: (0, j)),
        out_shape=jax.ShapeDtypeStruct((8, n), F32),
        compiler_params=_cparams(("parallel",)),
        name="adaln",
    )(cond, w, b.reshape(1, n))


def _norm_modulate(x, ng, sh, sc):
    ms = jnp.mean(x * x, axis=-1, keepdims=True)
    return (x * lax.rsqrt(ms + EPS) * ng) * (1.0 + sc) + sh


def _qk_norm_rope(acc, col0, g_ref, bd_ref, cos_ref, sin_ref, out_ref, rope, scale):
    for hb in range(acc.shape[1] // HEAD_W):
        t = acc[:, hb * HEAD_W:(hb + 1) * HEAD_W]
        ms = jnp.dot((t * t).astype(BF16), bd_ref[...], preferred_element_type=F32)
        t = t * lax.rsqrt(ms + EPS) * g_ref[...]
        if rope:
            lane = lax.broadcasted_iota(jnp.int32, t.shape, 1)
            r = jnp.where((lane & 31) < 16, pltpu.roll(t, LANES - 16, 1), pltpu.roll(t, 16, 1))
            t = t * cos_ref[...] + r * sin_ref[...]
        if scale != 1.0:
            t = t * scale
        out_ref[:, col0 + hb * HEAD_W:col0 + (hb + 1) * HEAD_W] = t.astype(BF16)


def _even_in_body(x_ref, sh_ref, sc_ref, ng_ref, w_ref, vng_ref, ws_ref, bsb_ref, gq_ref, gk_ref,
                  cos_ref, sin_ref, bd_ref, *rest, full, tm, q_scale):
    if full:
        ya_ref, q_ref, k_ref, v_ref, sbg_ref, h_s, au_s, gv_s, vn_s = rest
    else:
        k_ref, v_ref, h_s = rest
    sec0 = 0 if full else 4
    col_chunks = range(0, SEC_W, MXU_N)

    h_s[...] = _norm_modulate(x_ref[...], ng_ref[...], sh_ref[...], sc_ref[...]).astype(BF16)

    def proj(sec, c0):
        w0 = (sec - sec0) * SEC_W + c0
        return jnp.dot(h_s[...], w_ref[:, w0:w0 + MXU_N], preferred_element_type=F32)

    def proj_section(sec):
        w0 = (sec - sec0) * SEC_W
        return jnp.dot(h_s[...], w_ref[:, w0:w0 + SEC_W], preferred_element_type=F32)

    if full:
        for c0 in col_chunks:
            au_s[:, c0:c0 + MXU_N] = _gelu(proj(0, c0)).astype(BF16)

        ssq = jnp.zeros((tm, LANES), F32)
        for c0 in col_chunks:
            gv = _gelu(proj(1, c0))
            gv_s[:, c0:c0 + MXU_N] = gv
            for l in range(0, MXU_N, LANES):
                ssq = ssq + gv[:, l:l + LANES] * gv[:, l:l + LANES]
        inv = lax.rsqrt(jnp.sum(ssq, axis=-1, keepdims=True) * (1.0 / A_WIDTH) + EPS)
        vn_s[...] = (gv_s[...] * inv * vng_ref[...]).astype(BF16)

        for c0 in col_chunks:
            sg = _silu(proj(2, c0))
            for r0 in range(0, tm, CHUNK):
                rows = slice(r0, r0 + CHUNK)
                for l in range(0, MXU_N, LANES):
                    g = (c0 + l) // LANES
                    cols = slice(c0 + l, c0 + l + LANES)
                    mixed = jnp.dot(ws_ref[g], vn_s[rows, cols], preferred_element_type=F32) + bsb_ref[g]
                    ya_ref[rows, cols] = (au_s[rows, cols].astype(F32) * mixed * sg[rows, l:l + LANES]).astype(BF16)

        _qk_norm_rope(proj_section(3), 0, gq_ref, bd_ref, cos_ref, sin_ref, q_ref, True, q_scale)

        for c0 in col_chunks:
            sbg_ref[:, c0:c0 + MXU_N] = _silu(proj(6, c0)).astype(BF16)

    _qk_norm_rope(proj_section(4), 0, gk_ref, bd_ref, cos_ref, sin_ref, k_ref, full, 1.0)

    for c0 in col_chunks:
        v_ref[:, c0:c0 + MXU_N] = proj(5, c0).astype(BF16)


def _even_in(x2d, sh, sc, ng, w_bf, vng, ws_bf, bsb, gq, gk, cos, sin, bd, *, full, tm, rows_per_batch,
             q_scale):
    m = x2d.shape[0]
    tiles_per_batch = rows_per_batch // tm
    row = lambda i: (i, 0)
    const2 = lambda i: (0, 0)
    const3 = lambda i: (0, 0, 0)
    batch3 = lambda i: (i // tiles_per_batch, 0, 0)
    pos = lambda i: (i % tiles_per_batch, 0)
    if full:
        w_spec = _resident(w_bf.shape)
    else:
        w_spec = pl.BlockSpec((D_MODEL, 2 * SEC_W), lambda i: (0, 2), pipeline_mode=pl.Buffered(1))
    in_specs = [
        pl.BlockSpec((tm, D_MODEL), row),
        pl.BlockSpec((None, 1, D_MODEL), batch3),
        pl.BlockSpec((None, 1, D_MODEL), batch3),
        pl.BlockSpec((1, D_MODEL), const2),
        w_spec,
        pl.BlockSpec((1, A_WIDTH), const2),
        pl.BlockSpec((A_GROUPS, CHUNK, CHUNK), const3),
        pl.BlockSpec((A_GROUPS, CHUNK, LANES), const3),
        pl.BlockSpec((1, HEAD_W), const2),
        pl.BlockSpec((1, HEAD_W), const2),
        pl.BlockSpec((tm, HEAD_W), pos),
        pl.BlockSpec((tm, HEAD_W), pos),
        pl.BlockSpec((HEAD_W, HEAD_W), const2),
    ]
    sec_out = jax.ShapeDtypeStruct((m, SEC_W), BF16)
    n_out = 5 if full else 2
    scratch = [pltpu.VMEM((tm, D_MODEL), BF16)]
    if full:
        scratch += [pltpu.VMEM((tm, A_WIDTH), BF16), pltpu.VMEM((tm, A_WIDTH), F32),
                    pltpu.VMEM((tm, A_WIDTH), BF16)]
    return pl.pallas_call(
        functools.partial(_even_in_body, full=full, tm=tm, q_scale=q_scale),
        grid=(m // tm,),
        in_specs=in_specs,
        out_specs=[pl.BlockSpec((tm, SEC_W), row)] * n_out,
        out_shape=[sec_out] * n_out,
        scratch_shapes=scratch,
        compiler_params=_cparams(("parallel",)),
        name="even_in" if full else "even_in_ctx",
    )(x2d, sh, sc, ng, w_bf, vng, ws_bf, bsb, gq, gk, cos, sin, bd)


def _attn_body(q_ref, k_ref, kc_ref, v_ref, vc_ref, sbg_ref, lam_ref, og_ref, o_ref,
               kk_s, v1t_s, s_s, *, tq, tg, tk, n_lat, n_ctx, lambda_init):
    i = pl.program_id(2)
    n_tot = n_lat + n_ctx
    chunks = [(c0, min(tk, n_tot - c0)) for c0 in range(0, n_tot, tk)]

    @pl.when(i == 0)
    def _():
        kk_s[0:n_lat, :] = k_ref[...]
        kk_s[n_lat:n_tot, :] = kc_ref[...]
        for c0, w in chunks:
            src, r0 = (v_ref, c0) if c0 < n_lat else (vc_ref, c0 - n_lat)
            v1t_s[0:HEAD_W, c0:c0 + w] = src[r0:r0 + w, :].astype(F32).T.astype(BF16)
        v1t_s[HEAD_W:HEAD_W + BF16_ROWS, :] = jnp.ones((BF16_ROWS, n_tot), BF16)

    qt = q_ref[...].astype(F32).T
    row = lax.broadcasted_iota(jnp.int32, qt.shape, 0)
    qts = (jnp.where(row < B_QK_DIM, qt, 0.0).astype(BF16), jnp.where(row >= B_QK_DIM, qt, 0.0).astype(BF16))
    groups = [(t, r0) for r0 in range(0, tq, tg) for t in range(2)]
    slot0 = lambda g: (g % S_SLOTS) * n_tot

    def scores(g):
        t, r0 = groups[g]
        qg = qts[t][:, r0:r0 + tg]
        mx = jnp.full((1, tg), -jnp.inf, F32)
        for c0, w in chunks:
            sc = jnp.dot(kk_s[c0:c0 + w, :], qg, preferred_element_type=F32)
            s_s[slot0(g) + c0:slot0(g) + c0 + w, :] = sc
            mx = jnp.maximum(mx, jnp.max(sc, axis=0, keepdims=True))
        return mx

    def weighted_sum(g, mx):
        acc = None
        for c0, w in chunks:
            p = jnp.exp2(s_s[slot0(g) + c0:slot0(g) + c0 + w, :] - mx).astype(BF16)
            d = jnp.dot(v1t_s[:, c0:c0 + w], p, preferred_element_type=F32)
            acc = d if acc is None else acc + d
        return acc

    n_groups = len(groups)
    accs = [None] * n_groups
    mx_prev = scores(0)
    for g in range(1, n_groups):
        mx_next = scores(g)
        accs[g - 1] = weighted_sum(g - 1, mx_prev)
        mx_prev = mx_next
    accs[n_groups - 1] = weighted_sum(n_groups - 1, mx_prev)

    lv = lam_ref[...]
    lam = (jnp.exp(jnp.sum(lv[0:1, :] * lv[1:2, :], axis=-1, keepdims=True))
           - jnp.exp(jnp.sum(lv[2:3, :] * lv[3:4, :], axis=-1, keepdims=True)) + lambda_init)
    for n, r0 in enumerate(range(0, tq, tg)):
        a1, a2 = accs[2 * n], accs[2 * n + 1]
        ot = (a1[0:HEAD_W, :] / a1[HEAD_W:HEAD_W + 1, :]
              - lam * (a2[0:HEAD_W, :] / a2[HEAD_W:HEAD_W + 1, :]))
        ms = jnp.mean(ot * ot, axis=0, keepdims=True)
        o = (ot * lax.rsqrt(ms + EPS)).T
        y = (o * og_ref[...]) * (1.0 - lambda_init)
        o_ref[r0:r0 + tg, :] = (y * sbg_ref[r0:r0 + tg, :].astype(F32)).astype(BF16)


def _attention(q, k, kc, v, vc, sbg, lam_vecs, og, *, tq, tg, tk, lambda_init):
    bsz, n_lat, _ = q.shape
    n_ctx = kc.shape[1]
    n_tot = n_lat + n_ctx
    qtile = pl.BlockSpec((None, tq, HEAD_W), lambda b, h, i: (b, i, h))
    whole = lambda n: pl.BlockSpec((None, n, HEAD_W), lambda b, h, i: (b, 0, h))
    const2 = lambda b, h, i: (0, 0)
    return pl.pallas_call(
        functools.partial(_attn_body, tq=tq, tg=tg, tk=tk, n_lat=n_lat, n_ctx=n_ctx, lambda_init=lambda_init),
        grid=(bsz, B_HEADS, n_lat // tq),
        in_specs=[qtile, whole(n_lat), whole(n_ctx), whole(n_lat), whole(n_ctx), qtile,
                  pl.BlockSpec((4, B_QK_DIM), const2), pl.BlockSpec((1, HEAD_W), const2)],
        out_specs=qtile,
        out_shape=jax.ShapeDtypeStruct((bsz, n_lat, B_WIDTH), BF16),
        scratch_shapes=[pltpu.VMEM((n_tot, HEAD_W), BF16),
                        pltpu.VMEM((HEAD_W + BF16_ROWS, n_tot), BF16),
                        pltpu.VMEM((S_SLOTS * n_tot, tg), F32)],
        compiler_params=_cparams(("parallel", "parallel", "arbitrary")),
        name="diff_attn",
    )(q, k, kc, v, vc, sbg, lam_vecs, og)


def _even_out_body(ya_ref, yb_ref, wa_ref, wb_ref, x_ref, gt_ref, o_ref):
    y = (jnp.dot(ya_ref[...], wa_ref[...], preferred_element_type=F32)
         + jnp.dot(yb_ref[...], wb_ref[...], preferred_element_type=F32))
    o_ref[...] = x_ref[...] + gt_ref[...] * y


def _even_out(ya, yb, w_bf, x2d, gt, *, tm, rows_per_batch):
    m = x2d.shape[0]
    tiles_per_batch = rows_per_batch // tm
    row = lambda i: (i, 0)
    return pl.pallas_call(
        _even_out_body,
        grid=(m // tm,),
        in_specs=[pl.BlockSpec((tm, A_WIDTH), row), pl.BlockSpec((tm, B_WIDTH), row),
                  pl.BlockSpec((A_WIDTH, D_MODEL), lambda i: (0, 0)),
                  pl.BlockSpec((B_WIDTH, D_MODEL), lambda i: (1, 0)),
                  pl.BlockSpec((tm, D_MODEL), row),
                  pl.BlockSpec((None, 1, D_MODEL), lambda i: (i // tiles_per_batch, 0, 0))],
        out_specs=pl.BlockSpec((tm, D_MODEL), row),
        out_shape=jax.ShapeDtypeStruct((m, D_MODEL), F32),
        compiler_params=_cparams(("parallel",)),
        name="even_out",
    )(ya, yb, w_bf, w_bf, x2d, gt)


def _odd_in_body(x_ref, sh_ref, sc_ref, ng_ref, w_ref, y_ref, sg_ref, h_s):
    h_s[...] = _norm_modulate(x_ref[...], ng_ref[...], sh_ref[...], sc_ref[...]).astype(BF16)

    def proj(c0):
        return jnp.dot(h_s[...], w_ref[:, c0:c0 + MXU_N], preferred_element_type=F32)

    for c0 in range(0, D_MODEL, MXU_N):
        y_ref[:, c0:c0 + MXU_N] = (proj(c0) * jax.nn.sigmoid(proj(D_MODEL + c0))).astype(BF16)
        sg_ref[:, c0:c0 + MXU_N] = _silu(proj(2 * D_MODEL + c0)).astype(BF16)


def _odd_in(x2d, sh, sc, ng, w_bf, *, tm, rows_per_batch):
    m = x2d.shape[0]
    tiles_per_batch = rows_per_batch // tm
    row = lambda i: (i, 0)
    batch3 = lambda i: (i // tiles_per_batch, 0, 0)
    out = jax.ShapeDtypeStruct((m, D_MODEL), BF16)
    return pl.pallas_call(
        _odd_in_body,
        grid=(m // tm,),
        in_specs=[pl.BlockSpec((tm, D_MODEL), row),
                  pl.BlockSpec((None, 1, D_MODEL), batch3),
                  pl.BlockSpec((None, 1, D_MODEL), batch3),
                  pl.BlockSpec((1, D_MODEL), lambda i: (0, 0)),
                  _resident(w_bf.shape)],
        out_specs=[pl.BlockSpec((tm, D_MODEL), row)] * 2,
        out_shape=[out, out],
        scratch_shapes=[pltpu.VMEM((tm, D_MODEL), BF16)],
        compiler_params=_cparams(("parallel",)),
        name="odd_in",
    )(x2d, sh, sc, ng, w_bf)


def _odd_out_body(y_ref, yp_ref, yn_ref, sg_ref, x_ref, gt_ref, dw_ref, db_ref, lg_ref, lb_ref, w_ref,
                  o_ref, ybuf_s, conv_s, u_s, *, tm, tiles_per_batch):
    i = pl.program_id(0)
    t = i % tiles_per_batch
    n_slabs = D_MODEL // LANES
    for cb in range(n_slabs):
        cols = slice(cb * LANES, (cb + 1) * LANES)
        ybuf_s[cb, 0:HALO, :] = jnp.where(t > 0, yp_ref[:, cols].astype(F32), 0.0)
        ybuf_s[cb, HALO:HALO + tm, :] = y_ref[:, cols].astype(F32)
        ybuf_s[cb, HALO + tm:2 * HALO + tm, :] = jnp.where(t < tiles_per_batch - 1, yn_ref[:, cols].astype(F32), 0.0)

    span = SUBLANES * ROW_STRIDE
    spans_per_group = 4

    def col_block(cb, carry):
        c0 = pl.multiple_of(cb * LANES, LANES)
        bias = jnp.broadcast_to(db_ref[:, pl.ds(c0, LANES)], (SUBLANES, LANES))
        for g0 in range(0, tm // span, spans_per_group):
            accs = [[bias] * ROW_STRIDE for _ in range(spans_per_group)]
            taps = {}
            for j in range(CONV_W + ROW_STRIDE - 1):
                if j < CONV_W:
                    taps[j] = dw_ref[j, :, pl.ds(c0, LANES)]
                for n in range(spans_per_group):
                    start = (g0 + n) * span + j + HALO - CONV_PAD
                    rows = ybuf_s[cb, pl.ds(start, SUBLANES, stride=ROW_STRIDE), :]
                    for s in range(ROW_STRIDE):
                        if 0 <= j - s < CONV_W:
                            accs[n][s] = accs[n][s] + rows * taps[j - s]
            for n in range(spans_per_group):
                for s in range(ROW_STRIDE):
                    conv_s[cb, pl.ds((g0 + n) * span + s, SUBLANES, stride=ROW_STRIDE), :] = accs[n][s]
        return carry

    lax.fori_loop(0, n_slabs, col_block, 0)

    tot = conv_s[0]
    for cb in range(1, n_slabs):
        tot = tot + conv_s[cb]
    mu = jnp.sum(tot, axis=-1, keepdims=True) * (1.0 / D_MODEL)
    sq = jnp.zeros((tm, LANES), F32)
    for cb in range(n_slabs):
        d = conv_s[cb] - mu
        sq = sq + d * d
    inv = lax.rsqrt(jnp.sum(sq, axis=-1, keepdims=True) * (1.0 / D_MODEL) + EPS)
    for cb in range(n_slabs):
        cols = slice(cb * LANES, (cb + 1) * LANES)
        z = (conv_s[cb] - mu) * inv * lg_ref[:, cols] + lb_ref[:, cols]
        u_s[:, cols] = (_silu(z) * sg_ref[:, cols].astype(F32)).astype(BF16)
    o_ref[...] = x_ref[...] + gt_ref[...] * jnp.dot(u_s[...], w_ref[...], preferred_element_type=F32)


def _odd_out(y, sg, x2d, gt, dw8, dw_b, ln_g, ln_b, w_bf, *, tm, rows_per_batch):
    m = x2d.shape[0]
    tiles_per_batch = rows_per_batch // tm
    hpt = tm // HALO
    last_halo = m // HALO - 1
    row = lambda i: (i, 0)
    const2 = lambda i: (0, 0)
    return pl.pallas_call(
        functools.partial(_odd_out_body, tm=tm, tiles_per_batch=tiles_per_batch),
        grid=(m // tm,),
        in_specs=[pl.BlockSpec((tm, D_MODEL), row),
                  pl.BlockSpec((HALO, D_MODEL), lambda i: (jnp.maximum(i * hpt - 1, 0), 0)),
                  pl.BlockSpec((HALO, D_MODEL), lambda i: (jnp.minimum((i + 1) * hpt, last_halo), 0)),
                  pl.BlockSpec((tm, D_MODEL), row),
                  pl.BlockSpec((tm, D_MODEL), row),
                  pl.BlockSpec((None, 1, D_MODEL), lambda i: (i // tiles_per_batch, 0, 0)),
                  _resident(dw8.shape),
                  pl.BlockSpec((1, D_MODEL), const2),
                  pl.BlockSpec((1, D_MODEL), const2),
                  pl.BlockSpec((1, D_MODEL), const2),
                  _resident(w_bf.shape)],
        out_specs=pl.BlockSpec((tm, D_MODEL), row),
        out_shape=jax.ShapeDtypeStruct((m, D_MODEL), F32),
        scratch_shapes=[pltpu.VMEM((D_MODEL // LANES, tm + 2 * HALO, LANES), F32),
                        pltpu.VMEM((D_MODEL // LANES, tm, LANES), F32),
                        pltpu.VMEM((tm, D_MODEL), BF16)],
        compiler_params=_cparams(("parallel",)),
        name="odd_out",
    )(y, y, y, sg, x2d, gt, dw8, dw_b, ln_g, ln_b, w_bf)


def _rope_tables(n_lat):
    axis_dim = B_QK_DIM // 2
    inv = ROPE_BASE ** (-jnp.arange(0, axis_dim, 2, dtype=F32) / axis_dim)
    rows = n_lat // GRID_W
    r = jnp.repeat(jnp.arange(rows, dtype=F32), GRID_W)
    c = jnp.tile(jnp.arange(GRID_W, dtype=F32), rows)
    ar = r[:, None] * inv[None, :]
    ac = c[:, None] * inv[None, :]
    ang = jnp.concatenate([ar, ar, ac, ac], axis=-1)
    sign = jnp.where((jnp.arange(B_QK_DIM) % 32) < 16, -1.0, 1.0).astype(F32)
    cos = jnp.tile(jnp.cos(ang), (1, 2))
    sin = jnp.tile(jnp.sin(ang) * sign[None, :], (1, 2))
    return cos, sin


def kernel(x, c, ctx, c_ctx, e_norm_g, e_ada_w, e_ada_b, e_w_in, e_a_vnorm_g, e_a_ws, e_a_bs, e_b_qnorm_g, e_b_knorm_g, e_b_lambda, e_b_onorm_g, e_w_out, o_norm_g, o_ada_w, o_ada_b, o_w_in, o_dw_w, o_dw_b, o_ln_g, o_ln_b, o_w_out):
    bsz, n_lat, d = x.shape
    n_ctx = ctx.shape[1]
    x2d = x.reshape(bsz * n_lat, d)
    ctx2d = ctx.reshape(bsz * n_ctx, d)
    cond = jnp.concatenate([c, c_ctx[None, :], jnp.zeros((8 - bsz - 1, d), F32)], axis=0)
    row3 = lambda t: t.reshape(t.shape[0], 1, d)

    lambda_init = 0.8 - 0.6 * math.exp(-0.3 * 0)
    mod = _adaln(cond, e_ada_w[0], e_ada_b[0])
    sh, sc, gt = (row3(mod[:bsz, n * d:(n + 1) * d]) for n in range(3))
    csh, csc = (jnp.broadcast_to(mod[bsz:bsz + 1, n * d:(n + 1) * d], (bsz, d)).reshape(bsz, 1, d)
                for n in range(2))
    cos, sin = _rope_tables(n_lat)
    gid = jnp.arange(HEAD_W) // B_QK_DIM
    bd = jnp.where(gid[:, None] == gid[None, :], 1.0 / B_QK_DIM, 0.0).astype(BF16)
    tile2 = lambda g: jnp.tile(g, 2).reshape(1, HEAD_W)
    w_in_bf = e_w_in[0].astype(BF16)
    common = (e_norm_g[0].reshape(1, d), w_in_bf, e_a_vnorm_g[0].reshape(1, A_WIDTH),
              e_a_ws[0].astype(BF16),
              jnp.broadcast_to(e_a_bs[0][:, :, None], (A_GROUPS, CHUNK, LANES)),
              tile2(e_b_qnorm_g[0]), tile2(e_b_knorm_g[0]), cos, sin, bd)
    q_scale = (B_QK_DIM ** -0.5) * LOG2E
    ya, q, k, v, sbg = _even_in(x2d, sh, sc, *common, full=True, tm=256, rows_per_batch=n_lat,
                                q_scale=q_scale)
    kc, vc = _even_in(ctx2d, csh, csc, *common, full=False, tm=n_ctx, rows_per_batch=n_ctx,
                      q_scale=q_scale)
    seq = lambda t, n: t.reshape(bsz, n, B_WIDTH)
    yb = _attention(seq(q, n_lat), seq(k, n_lat), seq(kc, n_ctx), seq(v, n_lat), seq(vc, n_ctx),
                    seq(sbg, n_lat), e_b_lambda[0], e_b_onorm_g[0].reshape(1, HEAD_W),
                    tq=2048, tg=256, tk=512, lambda_init=lambda_init)
    x1 = _even_out(ya, yb.reshape(bsz * n_lat, B_WIDTH), e_w_out[0].astype(BF16), x2d, gt,
                   tm=512, rows_per_batch=n_lat)

    mod = _adaln(cond, o_ada_w[0], o_ada_b[0])
    sh, sc, gt = (row3(mod[:bsz, n * d:(n + 1) * d]) for n in range(3))
    y, sg = _odd_in(x1, sh, sc, o_norm_g[0].reshape(1, d), o_w_in[0].astype(BF16),
                    tm=512, rows_per_batch=n_lat)
    dw8 = jnp.broadcast_to(o_dw_w[0][:, None, :], (CONV_W, SUBLANES, d))
    x2 = _odd_out(y, sg, x1, gt, dw8, o_dw_b[0].reshape(1, d), o_ln_g[0].reshape(1, d),
                  o_ln_b[0].reshape(1, d), o_w_out[0].astype(BF16), tm=512, rows_per_batch=n_lat)
    return x2.reshape(bsz, n_lat, d)
```

```python
import functools
import math

import jax
import jax.numpy as jnp
from jax import lax
from jax.experimental import pallas as pl
from jax.experimental.pallas import tpu as pltpu

F32 = jnp.float32
BF16 = jnp.bfloat16

D_MODEL = 2048
GRID_W = 64
EPS = 1e-6
CHUNK = 128
A_WIDTH = 1024
A_GROUPS = 8
B_HEADS = 8
B_QK_DIM = 64
HEAD_W = 128
B_WIDTH = 1024
ROPE_BASE = 10000.0
CONV_W = 31
CONV_PAD = 15
SEC_W = 1024
LANES = 128
SUBLANES = 8
BF16_ROWS = 16
MXU_N = 256
HALO = 16
ROW_STRIDE = 4
S_SLOTS = 3
VMEM_LIMIT = 56 * 1024 * 1024
LOG2E = 1.4426950408889634


def _cparams(sem):
    return pltpu.CompilerParams(dimension_semantics=sem, vmem_limit_bytes=VMEM_LIMIT)


def _resident(shape):
    return pl.BlockSpec(shape, lambda *_: (0,) * len(shape), pipeline_mode=pl.Buffered(1))


def _silu(x):
    return x * jax.nn.sigmoid(x)


def _gelu(x):
    return 0.5 * x * (1.0 + lax.erf(x * (2.0 ** -0.5)))


def _adaln_body(c_ref, w_ref, b_ref, o_ref):
    s = _silu(c_ref[...]).astype(BF16)
    o_ref[...] = jnp.dot(s, w_ref[...].astype(BF16), preferred_element_type=F32) + b_ref[...]


def _adaln(cond, w, b):
    tn = 512
    n = w.shape[1]
    return pl.pallas_call(
        _adaln_body,
        grid=(n // tn,),
        in_specs=[pl.BlockSpec((8, D_MODEL), lambda j: (0, 0)),
                  pl.BlockSpec((D_MODEL, tn), lambda j: (0, j)),
                  pl.BlockSpec((1, tn), lambda j: (0, j))],
        out_specs=pl.BlockSpec((8, tn), lambda j: (0, j)),
        out_shape=jax.ShapeDtypeStruct((8, n), F32),
        compiler_params=_cparams(("parallel",)),
        name="adaln",
    )(cond, w, b.reshape(1, n))


def _norm_modulate(x, ng, sh, sc):
    ms = jnp.mean(x * x, axis=-1, keepdims=True)
    return (x * lax.rsqrt(ms + EPS) * ng) * (1.0 + sc) + sh


def _qk_norm_rope(acc, col0, g_ref, bd_ref, cos_ref, sin_ref, out_ref, rope, scale):
    for hb in range(acc.shape[1] // HEAD_W):
        t = acc[:, hb * HEAD_W:(hb + 1) * HEAD_W]
        ms = jnp.dot((t * t).astype(BF16), bd_ref[...], preferred_element_type=F32)
        t = t * lax.rsqrt(ms + EPS) * g_ref[...]
        if rope:
            lane = lax.broadcasted_iota(jnp.int32, t.shape, 1)
            r = jnp.where((lane & 31) < 16, pltpu.roll(t, LANES - 16, 1), pltpu.roll(t, 16, 1))
            t = t * cos_ref[...] + r * sin_ref[...]
        if scale != 1.0:
            t = t * scale
        out_ref[:, col0 + hb * HEAD_W:col0 + (hb + 1) * HEAD_W] = t.astype(BF16)


def _even_in_body(x_ref, sh_ref, sc_ref, ng_ref, w_ref, vng_ref, ws_ref, bsb_ref, gq_ref, gk_ref,
                  cos_ref, sin_ref, bd_ref, *rest, full, tm, q_scale):
    if full:
        ya_ref, q_ref, k_ref, v_ref, sbg_ref, h_s, au_s, gv_s, vn_s = rest
    else:
        k_ref, v_ref, h_s = rest
    sec0 = 0 if full else 4
    col_chunks = range(0, SEC_W, MXU_N)

    h_s[...] = _norm_modulate(x_ref[...], ng_ref[...], sh_ref[...], sc_ref[...]).astype(BF16)

    def proj(sec, c0):
        w0 = (sec - sec0) * SEC_W + c0
        return jnp.dot(h_s[...], w_ref[:, w0:w0 + MXU_N], preferred_element_type=F32)

    def proj_section(sec):
        w0 = (sec - sec0) * SEC_W
        return jnp.dot(h_s[...], w_ref[:, w0:w0 + SEC_W], preferred_element_type=F32)

    if full:
        for c0 in col_chunks:
            au_s[:, c0:c0 + MXU_N] = _gelu(proj(0, c0)).astype(BF16)

        ssq = jnp.zeros((tm, LANES), F32)
        for c0 in col_chunks:
            gv = _gelu(proj(1, c0))
            gv_s[:, c0:c0 + MXU_N] = gv
            for l in range(0, MXU_N, LANES):
                ssq = ssq + gv[:, l:l + LANES] * gv[:, l:l + LANES]
        inv = lax.rsqrt(jnp.sum(ssq, axis=-1, keepdims=True) * (1.0 / A_WIDTH) + EPS)
        vn_s[...] = (gv_s[...] * inv * vng_ref[...]).astype(BF16)

        for c0 in col_chunks:
            sg = _silu(proj(2, c0))
            for r0 in range(0, tm, CHUNK):
                rows = slice(r0, r0 + CHUNK)
                for l in range(0, MXU_N, LANES):
                    g = (c0 + l) // LANES
                    cols = slice(c0 + l, c0 + l + LANES)
                    mixed = jnp.dot(ws_ref[g], vn_s[rows, cols], preferred_element_type=F32) + bsb_ref[g]
                    ya_ref[rows, cols] = (au_s[rows, cols].astype(F32) * mixed * sg[rows, l:l + LANES]).astype(BF16)

        _qk_norm_rope(proj_section(3), 0, gq_ref, bd_ref, cos_ref, sin_ref, q_ref, True, q_scale)

        for c0 in col_chunks:
            sbg_ref[:, c0:c0 + MXU_N] = _silu(proj(6, c0)).astype(BF16)

    _qk_norm_rope(proj_section(4), 0, gk_ref, bd_ref, cos_ref, sin_ref, k_ref, full, 1.0)

    for c0 in col_chunks:
        v_ref[:, c0:c0 + MXU_N] = proj(5, c0).astype(BF16)


def _even_in(x2d, sh, sc, ng, w_bf, vng, ws_bf, bsb, gq, gk, cos, sin, bd, *, full, tm, rows_per_batch,
             q_scale):
    m = x2d.shape[0]
    tiles_per_batch = rows_per_batch // tm
    row = lambda i: (i, 0)
    const2 = lambda i: (0, 0)
    const3 = lambda i: (0, 0, 0)
    batch3 = lambda i: (i // tiles_per_batch, 0, 0)
    pos = lambda i: (i % tiles_per_batch, 0)
    if full:
        w_spec = _resident(w_bf.shape)
    else:
        w_spec = pl.BlockSpec((D_MODEL, 2 * SEC_W), lambda i: (0, 2), pipeline_mode=pl.Buffered(1))
    in_specs = [
        pl.BlockSpec((tm, D_MODEL), row),
        pl.BlockSpec((None, 1, D_MODEL), batch3),
        pl.BlockSpec((None, 1, D_MODEL), batch3),
        pl.BlockSpec((1, D_MODEL), const2),
        w_spec,
        pl.BlockSpec((1, A_WIDTH), const2),
        pl.BlockSpec((A_GROUPS, CHUNK, CHUNK), const3),
        pl.BlockSpec((A_GROUPS, CHUNK, LANES), const3),
        pl.BlockSpec((1, HEAD_W), const2),
        pl.BlockSpec((1, HEAD_W), const2),
        pl.BlockSpec((tm, HEAD_W), pos),
        pl.BlockSpec((tm, HEAD_W), pos),
        pl.BlockSpec((HEAD_W, HEAD_W), const2),
    ]
    sec_out = jax.ShapeDtypeStruct((m, SEC_W), BF16)
    n_out = 5 if full else 2
    scratch = [pltpu.VMEM((tm, D_MODEL), BF16)]
    if full:
        scratch += [pltpu.VMEM((tm, A_WIDTH), BF16), pltpu.VMEM((tm, A_WIDTH), F32),
                    pltpu.VMEM((tm, A_WIDTH), BF16)]
    return pl.pallas_call(
        functools.partial(_even_in_body, full=full, tm=tm, q_scale=q_scale),
        grid=(m // tm,),
        in_specs=in_specs,
        out_specs=[pl.BlockSpec((tm, SEC_W), row)] * n_out,
        out_shape=[sec_out] * n_out,
        scratch_shapes=scratch,
        compiler_params=_cparams(("parallel",)),
        name="even_in" if full else "even_in_ctx",
    )(x2d, sh, sc, ng, w_bf, vng, ws_bf, bsb, gq, gk, cos, sin, bd)


def _attn_body(q_ref, k_ref, kc_ref, v_ref, vc_ref, sbg_ref, lam_ref, og_ref, o_ref,
               kk_s, v1t_s, s_s, *, tq, tg, tk, n_lat, n_ctx, lambda_init):
    i = pl.program_id(2)
    n_tot = n_lat + n_ctx
    chunks = [(c0, min(tk, n_tot - c0)) for c0 in range(0, n_tot, tk)]

    @pl.when(i == 0)
    def _():
        kk_s[0:n_lat, :] = k_ref[...]
        kk_s[n_lat:n_tot, :] = kc_ref[...]
        for c0, w in chunks:
            src, r0 = (v_ref, c0) if c0 < n_lat else (vc_ref, c0 - n_lat)
            v1t_s[0:HEAD_W, c0:c0 + w] = src[r0:r0 + w, :].astype(F32).T.astype(BF16)
        v1t_s[HEAD_W:HEAD_W + BF16_ROWS, :] = jnp.ones((BF16_ROWS, n_tot), BF16)

    qt = q_ref[...].astype(F32).T
    row = lax.broadcasted_iota(jnp.int32, qt.shape, 0)
    qts = (jnp.where(row < B_QK_DIM, qt, 0.0).astype(BF16), jnp.where(row >= B_QK_DIM, qt, 0.0).astype(BF16))
    groups = [(t, r0) for r0 in range(0, tq, tg) for t in range(2)]
    slot0 = lambda g: (g % S_SLOTS) * n_tot

    def scores(g):
        t, r0 = groups[g]
        qg = qts[t][:, r0:r0 + tg]
        mx = jnp.full((1, tg), -jnp.inf, F32)
        for c0, w in chunks:
            sc = jnp.dot(kk_s[c0:c0 + w, :], qg, preferred_element_type=F32)
            s_s[slot0(g) + c0:slot0(g) + c0 + w, :] = sc
            mx = jnp.maximum(mx, jnp.max(sc, axis=0, keepdims=True))
        return mx

    def weighted_sum(g, mx):
        acc = None
        for c0, w in chunks:
            p = jnp.exp2(s_s[slot0(g) + c0:slot0(g) + c0 + w, :] - mx).astype(BF16)
            d = jnp.dot(v1t_s[:, c0:c0 + w], p, preferred_element_type=F32)
            acc = d if acc is None else acc + d
        return acc

    n_groups = len(groups)
    accs = [None] * n_groups
    mx_prev = scores(0)
    for g in range(1, n_groups):
        mx_next = scores(g)
        accs[g - 1] = weighted_sum(g - 1, mx_prev)
        mx_prev = mx_next
    accs[n_groups - 1] = weighted_sum(n_groups - 1, mx_prev)

    lv = lam_ref[...]
    lam = (jnp.exp(jnp.sum(lv[0:1, :] * lv[1:2, :], axis=-1, keepdims=True))
           - jnp.exp(jnp.sum(lv[2:3, :] * lv[3:4, :], axis=-1, keepdims=True)) + lambda_init)
    for n, r0 in enumerate(range(0, tq, tg)):
        a1, a2 = accs[2 * n], accs[2 * n + 1]
        ot = (a1[0:HEAD_W, :] / a1[HEAD_W:HEAD_W + 1, :]
              - lam * (a2[0:HEAD_W, :] / a2[HEAD_W:HEAD_W + 1, :]))
        ms = jnp.mean(ot * ot, axis=0, keepdims=True)
        o = (ot * lax.rsqrt(ms + EPS)).T
        y = (o * og_ref[...]) * (1.0 - lambda_init)
        o_ref[r0:r0 + tg, :] = (y * sbg_ref[r0:r0 + tg, :].astype(F32)).astype(BF16)


def _attention(q, k, kc, v, vc, sbg, lam_vecs, og, *, tq, tg, tk, lambda_init):
    bsz, n_lat, _ = q.shape
    n_ctx = kc.shape[1]
    n_tot = n_lat + n_ctx
    qtile = pl.BlockSpec((None, tq, HEAD_W), lambda b, h, i: (b, i, h))
    whole = lambda n: pl.BlockSpec((None, n, HEAD_W), lambda b, h, i: (b, 0, h))
    const2 = lambda b, h, i: (0, 0)
    return pl.pallas_call(
        functools.partial(_attn_body, tq=tq, tg=tg, tk=tk, n_lat=n_lat, n_ctx=n_ctx, lambda_init=lambda_init),
        grid=(bsz, B_HEADS, n_lat // tq),
        in_specs=[qtile, whole(n_lat), whole(n_ctx), whole(n_lat), whole(n_ctx), qtile,
                  pl.BlockSpec((4, B_QK_DIM), const2), pl.BlockSpec((1, HEAD_W), const2)],
        out_specs=qtile,
        out_shape=jax.ShapeDtypeStruct((bsz, n_lat, B_WIDTH), BF16),
        scratch_shapes=[pltpu.VMEM((n_tot, HEAD_W), BF16),
                        pltpu.VMEM((HEAD_W + BF16_ROWS, n_tot), BF16),
                        pltpu.VMEM((S_SLOTS * n_tot, tg), F32)],
        compiler_params=_cparams(("parallel", "parallel", "arbitrary")),
        name="diff_attn",
    )(q, k, kc, v, vc, sbg, lam_vecs, og)


def _mid_body(ya_ref, yb_ref, wo_ref, x_ref, gt_ref, sh_ref, sc_ref, ng_ref, w_ref,
              x1_ref, y_ref, sg_ref, h_s, *, tm):
    col_chunks = range(0, D_MODEL, MXU_N)
    ssq = jnp.zeros((tm, LANES), F32)
    for c0 in col_chunks:
        cols = slice(c0, c0 + MXU_N)
        acc = (jnp.dot(ya_ref[...], wo_ref[0:A_WIDTH, cols], preferred_element_type=F32)
               + jnp.dot(yb_ref[...], wo_ref[A_WIDTH:A_WIDTH + B_WIDTH, cols], preferred_element_type=F32))
        x1 = x_ref[:, cols] + gt_ref[:, cols] * acc
        x1_ref[:, cols] = x1
        for l in range(0, MXU_N, LANES):
            ssq = ssq + x1[:, l:l + LANES] * x1[:, l:l + LANES]
    inv = lax.rsqrt(jnp.sum(ssq, axis=-1, keepdims=True) * (1.0 / D_MODEL) + EPS)
    h_s[...] = ((x1_ref[...] * inv * ng_ref[...]) * (1.0 + sc_ref[...]) + sh_ref[...]).astype(BF16)

    def proj(c0):
        return jnp.dot(h_s[...], w_ref[:, c0:c0 + MXU_N], preferred_element_type=F32)

    for c0 in col_chunks:
        y_ref[:, c0:c0 + MXU_N] = (proj(c0) * jax.nn.sigmoid(proj(D_MODEL + c0))).astype(BF16)
        sg_ref[:, c0:c0 + MXU_N] = _silu(proj(2 * D_MODEL + c0)).astype(BF16)


def _mid(ya, yb, wo_bf, x2d, gt, sh, sc, ng, w_bf, *, tm, rows_per_batch):
    m = x2d.shape[0]
    tiles_per_batch = rows_per_batch // tm
    row = lambda i: (i, 0)
    batch3 = lambda i: (i // tiles_per_batch, 0, 0)
    act = jax.ShapeDtypeStruct((m, D_MODEL), BF16)
    return pl.pallas_call(
        functools.partial(_mid_body, tm=tm),
        grid=(m // tm,),
        in_specs=[pl.BlockSpec((tm, A_WIDTH), row), pl.BlockSpec((tm, B_WIDTH), row),
                  _resident(wo_bf.shape),
                  pl.BlockSpec((tm, D_MODEL), row),
                  pl.BlockSpec((None, 1, D_MODEL), batch3),
                  pl.BlockSpec((None, 1, D_MODEL), batch3),
                  pl.BlockSpec((None, 1, D_MODEL), batch3),
                  pl.BlockSpec((1, D_MODEL), lambda i: (0, 0)),
                  _resident(w_bf.shape)],
        out_specs=[pl.BlockSpec((tm, D_MODEL), row)] * 3,
        out_shape=[jax.ShapeDtypeStruct((m, D_MODEL), F32), act, act],
        scratch_shapes=[pltpu.VMEM((tm, D_MODEL), BF16)],
        compiler_params=_cparams(("parallel",)),
        name="mid",
    )(ya, yb, wo_bf, x2d, gt, sh, sc, ng, w_bf)


def _odd_out_body(y_ref, yp_ref, yn_ref, sg_ref, x_ref, gt_ref, dw_ref, db_ref, lg_ref, lb_ref, w_ref,
                  o_ref, ybuf_s, conv_s, u_s, *, tm, tiles_per_batch):
    i = pl.program_id(0)
    t = i % tiles_per_batch
    n_slabs = D_MODEL // LANES
    for cb in range(n_slabs):
        cols = slice(cb * LANES, (cb + 1) * LANES)
        ybuf_s[cb, 0:HALO, :] = jnp.where(t > 0, yp_ref[:, cols].astype(F32), 0.0)
        ybuf_s[cb, HALO:HALO + tm, :] = y_ref[:, cols].astype(F32)
        ybuf_s[cb, HALO + tm:2 * HALO + tm, :] = jnp.where(t < tiles_per_batch - 1, yn_ref[:, cols].astype(F32), 0.0)

    span = SUBLANES * ROW_STRIDE
    spans_per_group = 4

    def col_block(cb, carry):
        c0 = pl.multiple_of(cb * LANES, LANES)
        bias = jnp.broadcast_to(db_ref[:, pl.ds(c0, LANES)], (SUBLANES, LANES))
        for g0 in range(0, tm // span, spans_per_group):
            accs = [[bias] * ROW_STRIDE for _ in range(spans_per_group)]
            taps = {}
            for j in range(CONV_W + ROW_STRIDE - 1):
                if j < CONV_W:
                    taps[j] = dw_ref[j, :, pl.ds(c0, LANES)]
                for n in range(spans_per_group):
                    start = (g0 + n) * span + j + HALO - CONV_PAD
                    rows = ybuf_s[cb, pl.ds(start, SUBLANES, stride=ROW_STRIDE), :]
                    for s in range(ROW_STRIDE):
                        if 0 <= j - s < CONV_W:
                            accs[n][s] = accs[n][s] + rows * taps[j - s]
            for n in range(spans_per_group):
                for s in range(ROW_STRIDE):
                    conv_s[cb, pl.ds((g0 + n) * span + s, SUBLANES, stride=ROW_STRIDE), :] = accs[n][s]
        return carry

    lax.fori_loop(0, n_slabs, col_block, 0)

    tot = conv_s[0]
    for cb in range(1, n_slabs):
        tot = tot + conv_s[cb]
    mu = jnp.sum(tot, axis=-1, keepdims=True) * (1.0 / D_MODEL)
    sq = jnp.zeros((tm, LANES), F32)
    for cb in range(n_slabs):
        d = conv_s[cb] - mu
        sq = sq + d * d
    inv = lax.rsqrt(jnp.sum(sq, axis=-1, keepdims=True) * (1.0 / D_MODEL) + EPS)
    for cb in range(n_slabs):
        cols = slice(cb * LANES, (cb + 1) * LANES)
        z = (conv_s[cb] - mu) * inv * lg_ref[:, cols] + lb_ref[:, cols]
        u_s[:, cols] = (_silu(z) * sg_ref[:, cols].astype(F32)).astype(BF16)
    o_ref[...] = x_ref[...] + gt_ref[...] * jnp.dot(u_s[...], w_ref[...], preferred_element_type=F32)


def _odd_out(y, sg, x2d, gt, dw8, dw_b, ln_g, ln_b, w_bf, *, tm, rows_per_batch):
    m = x2d.shape[0]
    tiles_per_batch = rows_per_batch // tm
    hpt = tm // HALO
    last_halo = m // HALO - 1
    row = lambda i: (i, 0)
    const2 = lambda i: (0, 0)
    return pl.pallas_call(
        functools.partial(_odd_out_body, tm=tm, tiles_per_batch=tiles_per_batch),
        grid=(m // tm,),
        in_specs=[pl.BlockSpec((tm, D_MODEL), row),
                  pl.BlockSpec((HALO, D_MODEL), lambda i: (jnp.maximum(i * hpt - 1, 0), 0)),
                  pl.BlockSpec((HALO, D_MODEL), lambda i: (jnp.minimum((i + 1) * hpt, last_halo), 0)),
                  pl.BlockSpec((tm, D_MODEL), row),
                  pl.BlockSpec((tm, D_MODEL), row),
                  pl.BlockSpec((None, 1, D_MODEL), lambda i: (i // tiles_per_batch, 0, 0)),
                  _resident(dw8.shape),
                  pl.BlockSpec((1, D_MODEL), const2),
                  pl.BlockSpec((1, D_MODEL), const2),
                  pl.BlockSpec((1, D_MODEL), const2),
                  _resident(w_bf.shape)],
        out_specs=pl.BlockSpec((tm, D_MODEL), row),
        out_shape=jax.ShapeDtypeStruct((m, D_MODEL), F32),
        scratch_shapes=[pltpu.VMEM((D_MODEL // LANES, tm + 2 * HALO, LANES), F32),
                        pltpu.VMEM((D_MODEL // LANES, tm, LANES), F32),
                        pltpu.VMEM((tm, D_MODEL), BF16)],
        compiler_params=_cparams(("parallel",)),
        name="odd_out",
    )(y, y, y, sg, x2d, gt, dw8, dw_b, ln_g, ln_b, w_bf)


def _rope_tables(n_lat):
    axis_dim = B_QK_DIM // 2
    inv = ROPE_BASE ** (-jnp.arange(0, axis_dim, 2, dtype=F32) / axis_dim)
    rows = n_lat // GRID_W
    r = jnp.repeat(jnp.arange(rows, dtype=F32), GRID_W)
    c = jnp.tile(jnp.arange(GRID_W, dtype=F32), rows)
    ar = r[:, None] * inv[None, :]
    ac = c[:, None] * inv[None, :]
    ang = jnp.concatenate([ar, ar, ac, ac], axis=-1)
    sign = jnp.where((jnp.arange(B_QK_DIM) % 32) < 16, -1.0, 1.0).astype(F32)
    cos = jnp.tile(jnp.cos(ang), (1, 2))
    sin = jnp.tile(jnp.sin(ang) * sign[None, :], (1, 2))
    return cos, sin


def kernel(x, c, ctx, c_ctx, e_norm_g, e_ada_w, e_ada_b, e_w_in, e_a_vnorm_g, e_a_ws, e_a_bs, e_b_qnorm_g, e_b_knorm_g, e_b_lambda, e_b_onorm_g, e_w_out, o_norm_g, o_ada_w, o_ada_b, o_w_in, o_dw_w, o_dw_b, o_ln_g, o_ln_b, o_w_out):
    bsz, n_lat, d = x.shape
    n_ctx = ctx.shape[1]
    x2d = x.reshape(bsz * n_lat, d)
    ctx2d = ctx.reshape(bsz * n_ctx, d)
    cond = jnp.concatenate([c, c_ctx[None, :], jnp.zeros((8 - bsz - 1, d), F32)], axis=0)
    row3 = lambda t: t.reshape(t.shape[0], 1, d)

    lambda_init = 0.8 - 0.6 * math.exp(-0.3 * 0)
    mod = _adaln(cond, e_ada_w[0], e_ada_b[0])
    sh, sc, gt = (row3(mod[:bsz, n * d:(n + 1) * d]) for n in range(3))
    csh, csc = (jnp.broadcast_to(mod[bsz:bsz + 1, n * d:(n + 1) * d], (bsz, d)).reshape(bsz, 1, d)
                for n in range(2))
    cos, sin = _rope_tables(n_lat)
    gid = jnp.arange(HEAD_W) // B_QK_DIM
    bd = jnp.where(gid[:, None] == gid[None, :], 1.0 / B_QK_DIM, 0.0).astype(BF16)
    tile2 = lambda g: jnp.tile(g, 2).reshape(1, HEAD_W)
    w_in_bf = e_w_in[0].astype(BF16)
    common = (e_norm_g[0].reshape(1, d), w_in_bf, e_a_vnorm_g[0].reshape(1, A_WIDTH),
              e_a_ws[0].astype(BF16),
              jnp.broadcast_to(e_a_bs[0][:, :, None], (A_GROUPS, CHUNK, LANES)),
              tile2(e_b_qnorm_g[0]), tile2(e_b_knorm_g[0]), cos, sin, bd)
    q_scale = (B_QK_DIM ** -0.5) * LOG2E
    ya, q, k, v, sbg = _even_in(x2d, sh, sc, *common, full=True, tm=256, rows_per_batch=n_lat,
                                q_scale=q_scale)
    kc, vc = _even_in(ctx2d, csh, csc, *common, full=False, tm=n_ctx, rows_per_batch=n_ctx,
                      q_scale=q_scale)
    seq = lambda t, n: t.reshape(bsz, n, B_WIDTH)
    yb = _attention(seq(q, n_lat), seq(k, n_lat), seq(kc, n_ctx), seq(v, n_lat), seq(vc, n_ctx),
                    seq(sbg, n_lat), e_b_lambda[0], e_b_onorm_g[0].reshape(1, HEAD_W),
                    tq=2048, tg=256, tk=512, lambda_init=lambda_init)

    mod = _adaln(cond, o_ada_w[0], o_ada_b[0])
    sh, sc, gt1 = (row3(mod[:bsz, n * d:(n + 1) * d]) for n in range(3))
    x1, y, sg = _mid(ya, yb.reshape(bsz * n_lat, B_WIDTH), e_w_out[0].astype(BF16), x2d, gt, sh, sc,
                     o_norm_g[0].reshape(1, d), o_w_in[0].astype(BF16), tm=256, rows_per_batch=n_lat)
    gt = gt1
    dw8 = jnp.broadcast_to(o_dw_w[0][:, None, :], (CONV_W, SUBLANES, d))
    x2 = _odd_out(y, sg, x1, gt, dw8, o_dw_b[0].reshape(1, d), o_ln_g[0].reshape(1, d),
                  o_ln_b[0].reshape(1, d), o_w_out[0].astype(BF16), tm=512, rows_per_batch=n_lat)
    return x2.reshape(bsz, n_lat, d)
```

```python
import functools
import math

import jax
import jax.numpy as jnp
from jax import lax
from jax.experimental import pallas as pl
from jax.experimental.pallas import tpu as pltpu

F32 = jnp.float32
BF16 = jnp.bfloat16

D_MODEL = 2048
GRID_W = 64
EPS = 1e-6
CHUNK = 128
A_WIDTH = 1024
A_GROUPS = 8
B_HEADS = 8
B_QK_DIM = 64
HEAD_W = 128
B_WIDTH = 1024
ROPE_BASE = 10000.0
CONV_W = 31
CONV_PAD = 15
SEC_W = 1024
LANES = 128
SUBLANES = 8
BF16_ROWS = 16
MXU_N = 256
HALO = 16
ROW_STRIDE = 4
S_SLOTS = 3
VMEM_LIMIT = 60 * 1024 * 1024
LOG2E = 1.4426950408889634


def _cparams(sem):
    return pltpu.CompilerParams(dimension_semantics=sem, vmem_limit_bytes=VMEM_LIMIT)


def _resident(shape):
    return pl.BlockSpec(shape, lambda *_: (0,) * len(shape), pipeline_mode=pl.Buffered(1))


def _silu(x):
    return x * jax.nn.sigmoid(x)


def _gelu(x):
    return 0.5 * x * (1.0 + lax.erf(x * (2.0 ** -0.5)))


def _adaln_body(c_ref, w_ref, b_ref, o_ref):
    s = _silu(c_ref[...]).astype(BF16)
    o_ref[...] = jnp.dot(s, w_ref[...].astype(BF16), preferred_element_type=F32) + b_ref[...]


def _adaln(cond, w, b):
    tn = 512
    n = w.shape[1]
    return pl.pallas_call(
        _adaln_body,
        grid=(n // tn,),
        in_specs=[pl.BlockSpec((8, D_MODEL), lambda j: (0, 0)),
                  pl.BlockSpec((D_MODEL, tn), lambda j: (0, j)),
                  pl.BlockSpec((1, tn), lambda j: (0, j))],
        out_specs=pl.BlockSpec((8, tn), lambda j: (0, j)),
        out_shape=jax.ShapeDtypeStruct((8, n), F32),
        compiler_params=_cparams(("parallel",)),
        name="adaln",
    )(cond, w, b.reshape(1, n))


def _norm_modulate(x, ng, sh, sc):
    ms = jnp.mean(x * x, axis=-1, keepdims=True)
    return (x * lax.rsqrt(ms + EPS) * ng) * (1.0 + sc) + sh


def _qk_norm_rope(acc, col0, g_ref, bd_ref, cos_ref, sin_ref, out_ref, rope, scale):
    for hb in range(acc.shape[1] // HEAD_W):
        t = acc[:, hb * HEAD_W:(hb + 1) * HEAD_W]
        ms = jnp.dot((t * t).astype(BF16), bd_ref[...], preferred_element_type=F32)
        t = t * lax.rsqrt(ms + EPS) * g_ref[...]
        if rope:
            lane = lax.broadcasted_iota(jnp.int32, t.shape, 1)
            r = jnp.where((lane & 31) < 16, pltpu.roll(t, LANES - 16, 1), pltpu.roll(t, 16, 1))
            t = t * cos_ref[...] + r * sin_ref[...]
        if scale != 1.0:
            t = t * scale
        out_ref[:, col0 + hb * HEAD_W:col0 + (hb + 1) * HEAD_W] = t.astype(BF16)


def _even_in_body(x_ref, sh_ref, sc_ref, ng_ref, w_ref, vng_ref, ws_ref, bsb_ref, gq_ref, gk_ref,
                  cos_ref, sin_ref, bd_ref, *rest, full, tm, q_scale):
    if full:
        ya_ref, q_ref, k_ref, v_ref, sbg_ref, h_s, au_s, gv_s, vn_s = rest
    else:
        k_ref, v_ref, h_s = rest
    sec0 = 0 if full else 4
    col_chunks = range(0, SEC_W, MXU_N)

    h_s[...] = _norm_modulate(x_ref[...], ng_ref[...], sh_ref[...], sc_ref[...]).astype(BF16)

    def proj(sec, c0):
        w0 = (sec - sec0) * SEC_W + c0
        return jnp.dot(h_s[...], w_ref[:, w0:w0 + MXU_N], preferred_element_type=F32)

    def proj_section(sec):
        w0 = (sec - sec0) * SEC_W
        return jnp.dot(h_s[...], w_ref[:, w0:w0 + SEC_W], preferred_element_type=F32)

    if full:
        for c0 in col_chunks:
            au_s[:, c0:c0 + MXU_N] = _gelu(proj(0, c0)).astype(BF16)

        ssq = jnp.zeros((tm, LANES), F32)
        for c0 in col_chunks:
            gv = _gelu(proj(1, c0))
            gv_s[:, c0:c0 + MXU_N] = gv
            for l in range(0, MXU_N, LANES):
                ssq = ssq + gv[:, l:l + LANES] * gv[:, l:l + LANES]
        inv = lax.rsqrt(jnp.sum(ssq, axis=-1, keepdims=True) * (1.0 / A_WIDTH) + EPS)
        vn_s[...] = (gv_s[...] * inv * vng_ref[...]).astype(BF16)

        for c0 in col_chunks:
            sg = _silu(proj(2, c0))
            for r0 in range(0, tm, CHUNK):
                rows = slice(r0, r0 + CHUNK)
                for l in range(0, MXU_N, LANES):
                    g = (c0 + l) // LANES
                    cols = slice(c0 + l, c0 + l + LANES)
                    mixed = jnp.dot(ws_ref[g], vn_s[rows, cols], preferred_element_type=F32) + bsb_ref[g]
                    ya_ref[rows, cols] = (au_s[rows, cols].astype(F32) * mixed * sg[rows, l:l + LANES]).astype(BF16)

        _qk_norm_rope(proj_section(3), 0, gq_ref, bd_ref, cos_ref, sin_ref, q_ref, True, q_scale)

        for c0 in col_chunks:
            sbg_ref[:, c0:c0 + MXU_N] = _silu(proj(6, c0)).astype(BF16)

    _qk_norm_rope(proj_section(4), 0, gk_ref, bd_ref, cos_ref, sin_ref, k_ref, full, 1.0)

    for c0 in col_chunks:
        v_ref[:, c0:c0 + MXU_N] = proj(5, c0).astype(BF16)


def _even_in(x2d, sh, sc, ng, w_bf, vng, ws_bf, bsb, gq, gk, cos, sin, bd, *, full, tm, rows_per_batch,
             q_scale):
    m = x2d.shape[0]
    tiles_per_batch = rows_per_batch // tm
    row = lambda i: (i, 0)
    const2 = lambda i: (0, 0)
    const3 = lambda i: (0, 0, 0)
    batch3 = lambda i: (i // tiles_per_batch, 0, 0)
    pos = lambda i: (i % tiles_per_batch, 0)
    if full:
        w_spec = _resident(w_bf.shape)
    else:
        w_spec = pl.BlockSpec((D_MODEL, 2 * SEC_W), lambda i: (0, 2), pipeline_mode=pl.Buffered(1))
    in_specs = [
        pl.BlockSpec((tm, D_MODEL), row),
        pl.BlockSpec((None, 1, D_MODEL), batch3),
        pl.BlockSpec((None, 1, D_MODEL), batch3),
        pl.BlockSpec((1, D_MODEL), const2),
        w_spec,
        pl.BlockSpec((1, A_WIDTH), const2),
        pl.BlockSpec((A_GROUPS, CHUNK, CHUNK), const3),
        pl.BlockSpec((A_GROUPS, CHUNK, LANES), const3),
        pl.BlockSpec((1, HEAD_W), const2),
        pl.BlockSpec((1, HEAD_W), const2),
        pl.BlockSpec((tm, HEAD_W), pos),
        pl.BlockSpec((tm, HEAD_W), pos),
        pl.BlockSpec((HEAD_W, HEAD_W), const2),
    ]
    sec_out = jax.ShapeDtypeStruct((m, SEC_W), BF16)
    n_out = 5 if full else 2
    scratch = [pltpu.VMEM((tm, D_MODEL), BF16)]
    if full:
        scratch += [pltpu.VMEM((tm, A_WIDTH), BF16), pltpu.VMEM((tm, A_WIDTH), F32),
                    pltpu.VMEM((tm, A_WIDTH), BF16)]
    return pl.pallas_call(
        functools.partial(_even_in_body, full=full, tm=tm, q_scale=q_scale),
        grid=(m // tm,),
        in_specs=in_specs,
        out_specs=[pl.BlockSpec((tm, SEC_W), row)] * n_out,
        out_shape=[sec_out] * n_out,
        scratch_shapes=scratch,
        compiler_params=_cparams(("parallel",)),
        name="even_in" if full else "even_in_ctx",
    )(x2d, sh, sc, ng, w_bf, vng, ws_bf, bsb, gq, gk, cos, sin, bd)


def _attn_body(q_ref, k_ref, kc_ref, v_ref, vc_ref, sbg_ref, lam_ref, og_ref, o_ref,
               kk_s, v1t_s, s_s, *, tq, tg, tk, n_lat, n_ctx, lambda_init):
    i = pl.program_id(2)
    n_tot = n_lat + n_ctx
    chunks = [(c0, min(tk, n_tot - c0)) for c0 in range(0, n_tot, tk)]

    @pl.when(i == 0)
    def _():
        kk_s[0:n_lat, :] = k_ref[...]
        kk_s[n_lat:n_tot, :] = kc_ref[...]
        for c0, w in chunks:
            src, r0 = (v_ref, c0) if c0 < n_lat else (vc_ref, c0 - n_lat)
            v1t_s[0:HEAD_W, c0:c0 + w] = src[r0:r0 + w, :].astype(F32).T.astype(BF16)
        v1t_s[HEAD_W:HEAD_W + BF16_ROWS, :] = jnp.ones((BF16_ROWS, n_tot), BF16)

    qt = q_ref[...].astype(F32).T
    row = lax.broadcasted_iota(jnp.int32, qt.shape, 0)
    qts = (jnp.where(row < B_QK_DIM, qt, 0.0).astype(BF16), jnp.where(row >= B_QK_DIM, qt, 0.0).astype(BF16))
    groups = [(t, r0) for r0 in range(0, tq, tg) for t in range(2)]
    slot0 = lambda g: (g % S_SLOTS) * n_tot

    def scores(g):
        t, r0 = groups[g]
        qg = qts[t][:, r0:r0 + tg]
        mx = jnp.full((1, tg), -jnp.inf, F32)
        for c0, w in chunks:
            sc = jnp.dot(kk_s[c0:c0 + w, :], qg, preferred_element_type=F32)
            s_s[slot0(g) + c0:slot0(g) + c0 + w, :] = sc
            mx = jnp.maximum(mx, jnp.max(sc, axis=0, keepdims=True))
        return mx

    def weighted_sum(g, mx):
        acc = None
        for c0, w in chunks:
            p = jnp.exp2(s_s[slot0(g) + c0:slot0(g) + c0 + w, :] - mx).astype(BF16)
            d = jnp.dot(v1t_s[:, c0:c0 + w], p, preferred_element_type=F32)
            acc = d if acc is None else acc + d
        return acc

    n_groups = len(groups)
    accs = [None] * n_groups
    mx_prev = scores(0)
    for g in range(1, n_groups):
        mx_next = scores(g)
        accs[g - 1] = weighted_sum(g - 1, mx_prev)
        mx_prev = mx_next
    accs[n_groups - 1] = weighted_sum(n_groups - 1, mx_prev)

    lv = lam_ref[...]
    lam = (jnp.exp(jnp.sum(lv[0:1, :] * lv[1:2, :], axis=-1, keepdims=True))
           - jnp.exp(jnp.sum(lv[2:3, :] * lv[3:4, :], axis=-1, keepdims=True)) + lambda_init)
    for n, r0 in enumerate(range(0, tq, tg)):
        a1, a2 = accs[2 * n], accs[2 * n + 1]
        ot = (a1[0:HEAD_W, :] / a1[HEAD_W:HEAD_W + 1, :]
              - lam * (a2[0:HEAD_W, :] / a2[HEAD_W:HEAD_W + 1, :]))
        ms = jnp.mean(ot * ot, axis=0, keepdims=True)
        o = (ot * lax.rsqrt(ms + EPS)).T
        y = (o * og_ref[...]) * (1.0 - lambda_init)
        o_ref[r0:r0 + tg, :] = (y * sbg_ref[r0:r0 + tg, :].astype(F32)).astype(BF16)


def _attention(q, k, kc, v, vc, sbg, lam_vecs, og, *, tq, tg, tk, lambda_init):
    bsz, n_lat, _ = q.shape
    n_ctx = kc.shape[1]
    n_tot = n_lat + n_ctx
    qtile = pl.BlockSpec((None, tq, HEAD_W), lambda b, h, i: (b, i, h))
    whole = lambda n: pl.BlockSpec((None, n, HEAD_W), lambda b, h, i: (b, 0, h))
    const2 = lambda b, h, i: (0, 0)
    return pl.pallas_call(
        functools.partial(_attn_body, tq=tq, tg=tg, tk=tk, n_lat=n_lat, n_ctx=n_ctx, lambda_init=lambda_init),
        grid=(bsz, B_HEADS, n_lat // tq),
        in_specs=[qtile, whole(n_lat), whole(n_ctx), whole(n_lat), whole(n_ctx), qtile,
                  pl.BlockSpec((4, B_QK_DIM), const2), pl.BlockSpec((1, HEAD_W), const2)],
        out_specs=qtile,
        out_shape=jax.ShapeDtypeStruct((bsz, n_lat, B_WIDTH), BF16),
        scratch_shapes=[pltpu.VMEM((n_tot, HEAD_W), BF16),
                        pltpu.VMEM((HEAD_W + BF16_ROWS, n_tot), BF16),
                        pltpu.VMEM((S_SLOTS * n_tot, tg), F32)],
        compiler_params=_cparams(("parallel", "parallel", "arbitrary")),
        name="diff_attn",
    )(q, k, kc, v, vc, sbg, lam_vecs, og)


def _mid_body(ya_ref, yb_ref, wo_ref, x_ref, gt_ref, sh_ref, sc_ref, ng_ref, w_ref,
              x1_ref, y_ref, sg_ref, h_s, *, tm):
    col_chunks = range(0, D_MODEL, MXU_N)
    ssq = jnp.zeros((tm, LANES), F32)
    for c0 in col_chunks:
        cols = slice(c0, c0 + MXU_N)
        acc = (jnp.dot(ya_ref[...], wo_ref[0:A_WIDTH, cols], preferred_element_type=F32)
               + jnp.dot(yb_ref[...], wo_ref[A_WIDTH:A_WIDTH + B_WIDTH, cols], preferred_element_type=F32))
        x1 = x_ref[:, cols] + gt_ref[:, cols] * acc
        x1_ref[:, cols] = x1
        for l in range(0, MXU_N, LANES):
            ssq = ssq + x1[:, l:l + LANES] * x1[:, l:l + LANES]
    inv = lax.rsqrt(jnp.sum(ssq, axis=-1, keepdims=True) * (1.0 / D_MODEL) + EPS)
    h_s[...] = ((x1_ref[...] * inv * ng_ref[...]) * (1.0 + sc_ref[...]) + sh_ref[...]).astype(BF16)

    def proj(c0):
        return jnp.dot(h_s[...], w_ref[:, c0:c0 + MXU_N], preferred_element_type=F32)

    for c0 in col_chunks:
        y_ref[:, c0:c0 + MXU_N] = (proj(c0) * jax.nn.sigmoid(proj(D_MODEL + c0))).astype(BF16)
        sg_ref[:, c0:c0 + MXU_N] = _silu(proj(2 * D_MODEL + c0)).astype(BF16)


def _mid(ya, yb, wo_bf, x2d, gt, sh, sc, ng, w_bf, *, tm, rows_per_batch):
    m = x2d.shape[0]
    tiles_per_batch = rows_per_batch // tm
    row = lambda i: (i, 0)
    batch3 = lambda i: (i // tiles_per_batch, 0, 0)
    act = jax.ShapeDtypeStruct((m, D_MODEL), BF16)
    return pl.pallas_call(
        functools.partial(_mid_body, tm=tm),
        grid=(m // tm,),
        in_specs=[pl.BlockSpec((tm, A_WIDTH), row), pl.BlockSpec((tm, B_WIDTH), row),
                  _resident(wo_bf.shape),
                  pl.BlockSpec((tm, D_MODEL), row),
                  pl.BlockSpec((None, 1, D_MODEL), batch3),
                  pl.BlockSpec((None, 1, D_MODEL), batch3),
                  pl.BlockSpec((None, 1, D_MODEL), batch3),
                  pl.BlockSpec((1, D_MODEL), lambda i: (0, 0)),
                  _resident(w_bf.shape)],
        out_specs=[pl.BlockSpec((tm, D_MODEL), row)] * 3,
        out_shape=[jax.ShapeDtypeStruct((m, D_MODEL), F32), act, act],
        scratch_shapes=[pltpu.VMEM((tm, D_MODEL), BF16)],
        compiler_params=_cparams(("parallel",)),
        name="mid",
    )(ya, yb, wo_bf, x2d, gt, sh, sc, ng, w_bf)


def _odd_out_body(y_ref, yp_ref, yn_ref, sg_ref, x_ref, gt_ref, dw_ref, db_ref, lg_ref, lb_ref, w_ref,
                  o_ref, ybuf_s, conv_s, u_s, *, tm, tiles_per_batch):
    i = pl.program_id(0)
    t = i % tiles_per_batch
    n_slabs = D_MODEL // LANES
    for cb in range(n_slabs):
        cols = slice(cb * LANES, (cb + 1) * LANES)
        ybuf_s[cb, 0:HALO, :] = jnp.where(t > 0, yp_ref[:, cols].astype(F32), 0.0)
        ybuf_s[cb, HALO:HALO + tm, :] = y_ref[:, cols].astype(F32)
        ybuf_s[cb, HALO + tm:2 * HALO + tm, :] = jnp.where(t < tiles_per_batch - 1, yn_ref[:, cols].astype(F32), 0.0)

    span = SUBLANES * ROW_STRIDE
    spans_per_group = 4

    def col_block(cb, carry):
        c0 = pl.multiple_of(cb * LANES, LANES)
        bias = jnp.broadcast_to(db_ref[:, pl.ds(c0, LANES)], (SUBLANES, LANES))
        for g0 in range(0, tm // span, spans_per_group):
            accs = [[bias] * ROW_STRIDE for _ in range(spans_per_group)]
            taps = {}
            for j in range(CONV_W + ROW_STRIDE - 1):
                if j < CONV_W:
                    taps[j] = dw_ref[j, :, pl.ds(c0, LANES)]
                for n in range(spans_per_group):
                    start = (g0 + n) * span + j + HALO - CONV_PAD
                    rows = ybuf_s[cb, pl.ds(start, SUBLANES, stride=ROW_STRIDE), :]
                    for s in range(ROW_STRIDE):
                        if 0 <= j - s < CONV_W:
                            accs[n][s] = accs[n][s] + rows * taps[j - s]
            for n in range(spans_per_group):
                for s in range(ROW_STRIDE):
                    conv_s[cb, pl.ds((g0 + n) * span + s, SUBLANES, stride=ROW_STRIDE), :] = accs[n][s]
        return carry

    lax.fori_loop(0, n_slabs, col_block, 0)

    tot = conv_s[0]
    for cb in range(1, n_slabs):
        tot = tot + conv_s[cb]
    mu = jnp.sum(tot, axis=-1, keepdims=True) * (1.0 / D_MODEL)
    sq = jnp.zeros((tm, LANES), F32)
    for cb in range(n_slabs):
        d = conv_s[cb] - mu
        sq = sq + d * d
    inv = lax.rsqrt(jnp.sum(sq, axis=-1, keepdims=True) * (1.0 / D_MODEL) + EPS)
    for cb in range(n_slabs):
        cols = slice(cb * LANES, (cb + 1) * LANES)
        z = (conv_s[cb] - mu) * inv * lg_ref[:, cols] + lb_ref[:, cols]
        u_s[:, cols] = (_silu(z) * sg_ref[:, cols].astype(F32)).astype(BF16)
    o_ref[...] = x_ref[...] + gt_ref[...] * jnp.dot(u_s[...], w_ref[...], preferred_element_type=F32)


def _odd_out(y, sg, x2d, gt, dw8, dw_b, ln_g, ln_b, w_bf, *, tm, rows_per_batch):
    m = x2d.shape[0]
    tiles_per_batch = rows_per_batch // tm
    hpt = tm // HALO
    last_halo = m // HALO - 1
    row = lambda i: (i, 0)
    const2 = lambda i: (0, 0)
    return pl.pallas_call(
        functools.partial(_odd_out_body, tm=tm, tiles_per_batch=tiles_per_batch),
        grid=(m // tm,),
        in_specs=[pl.BlockSpec((tm, D_MODEL), row),
                  pl.BlockSpec((HALO, D_MODEL), lambda i: (jnp.maximum(i * hpt - 1, 0), 0)),
                  pl.BlockSpec((HALO, D_MODEL), lambda i: (jnp.minimum((i + 1) * hpt, last_halo), 0)),
                  pl.BlockSpec((tm, D_MODEL), row),
                  pl.BlockSpec((tm, D_MODEL), row),
                  pl.BlockSpec((None, 1, D_MODEL), lambda i: (i // tiles_per_batch, 0, 0)),
                  _resident(dw8.shape),
                  pl.BlockSpec((1, D_MODEL), const2),
                  pl.BlockSpec((1, D_MODEL), const2),
                  pl.BlockSpec((1, D_MODEL), const2),
                  _resident(w_bf.shape)],
        out_specs=pl.BlockSpec((tm, D_MODEL), row),
        out_shape=jax.ShapeDtypeStruct((m, D_MODEL), F32),
        scratch_shapes=[pltpu.VMEM((D_MODEL // LANES, tm + 2 * HALO, LANES), F32),
                        pltpu.VMEM((D_MODEL // LANES, tm, LANES), F32),
                        pltpu.VMEM((tm, D_MODEL), BF16)],
        compiler_params=_cparams(("parallel",)),
        name="odd_out",
    )(y, y, y, sg, x2d, gt, dw8, dw_b, ln_g, ln_b, w_bf)


def _rope_tables(n_lat):
    axis_dim = B_QK_DIM // 2
    inv = ROPE_BASE ** (-jnp.arange(0, axis_dim, 2, dtype=F32) / axis_dim)
    rows = n_lat // GRID_W
    r = jnp.repeat(jnp.arange(rows, dtype=F32), GRID_W)
    c = jnp.tile(jnp.arange(GRID_W, dtype=F32), rows)
    ar = r[:, None] * inv[None, :]
    ac = c[:, None] * inv[None, :]
    ang = jnp.concatenate([ar, ar, ac, ac], axis=-1)
    sign = jnp.where((jnp.arange(B_QK_DIM) % 32) < 16, -1.0, 1.0).astype(F32)
    cos = jnp.tile(jnp.cos(ang), (1, 2))
    sin = jnp.tile(jnp.sin(ang) * sign[None, :], (1, 2))
    return cos, sin


def kernel(x, c, ctx, c_ctx, e_norm_g, e_ada_w, e_ada_b, e_w_in, e_a_vnorm_g, e_a_ws, e_a_bs, e_b_qnorm_g, e_b_knorm_g, e_b_lambda, e_b_onorm_g, e_w_out, o_norm_g, o_ada_w, o_ada_b, o_w_in, o_dw_w, o_dw_b, o_ln_g, o_ln_b, o_w_out):
    bsz, n_lat, d = x.shape
    n_ctx = ctx.shape[1]
    x2d = x.reshape(bsz * n_lat, d)
    ctx2d = ctx.reshape(bsz * n_ctx, d)
    cond = jnp.concatenate([c, c_ctx[None, :], jnp.zeros((8 - bsz - 1, d), F32)], axis=0)
    row3 = lambda t: t.reshape(t.shape[0], 1, d)

    lambda_init = 0.8 - 0.6 * math.exp(-0.3 * 0)
    mod = _adaln(cond, e_ada_w[0], e_ada_b[0])
    sh, sc, gt = (row3(mod[:bsz, n * d:(n + 1) * d]) for n in range(3))
    csh, csc = (jnp.broadcast_to(mod[bsz:bsz + 1, n * d:(n + 1) * d], (bsz, d)).reshape(bsz, 1, d)
                for n in range(2))
    cos, sin = _rope_tables(n_lat)
    gid = jnp.arange(HEAD_W) // B_QK_DIM
    bd = jnp.where(gid[:, None] == gid[None, :], 1.0 / B_QK_DIM, 0.0).astype(BF16)
    tile2 = lambda g: jnp.tile(g, 2).reshape(1, HEAD_W)
    w_in_bf = e_w_in[0].astype(BF16)
    common = (e_norm_g[0].reshape(1, d), w_in_bf, e_a_vnorm_g[0].reshape(1, A_WIDTH),
              e_a_ws[0].astype(BF16),
              jnp.broadcast_to(e_a_bs[0][:, :, None], (A_GROUPS, CHUNK, LANES)),
              tile2(e_b_qnorm_g[0]), tile2(e_b_knorm_g[0]), cos, sin, bd)
    q_scale = (B_QK_DIM ** -0.5) * LOG2E
    ya, q, k, v, sbg = _even_in(x2d, sh, sc, *common, full=True, tm=512, rows_per_batch=n_lat,
                                q_scale=q_scale)
    kc, vc = _even_in(ctx2d, csh, csc, *common, full=False, tm=n_ctx, rows_per_batch=n_ctx,
                      q_scale=q_scale)
    seq = lambda t, n: t.reshape(bsz, n, B_WIDTH)
    yb = _attention(seq(q, n_lat), seq(k, n_lat), seq(kc, n_ctx), seq(v, n_lat), seq(vc, n_ctx),
                    seq(sbg, n_lat), e_b_lambda[0], e_b_onorm_g[0].reshape(1, HEAD_W),
                    tq=2048, tg=256, tk=512, lambda_init=lambda_init)

    mod = _adaln(cond, o_ada_w[0], o_ada_b[0])
    sh, sc, gt1 = (row3(mod[:bsz, n * d:(n + 1) * d]) for n in range(3))
    x1, y, sg = _mid(ya, yb.reshape(bsz * n_lat, B_WIDTH), e_w_out[0].astype(BF16), x2d, gt, sh, sc,
                     o_norm_g[0].reshape(1, d), o_w_in[0].astype(BF16), tm=256, rows_per_batch=n_lat)
    gt = gt1
    dw8 = jnp.broadcast_to(o_dw_w[0][:, None, :], (CONV_W, SUBLANES, d))
    x2 = _odd_out(y, sg, x1, gt, dw8, o_dw_b[0].reshape(1, d), o_ln_g[0].reshape(1, d),
                  o_ln_b[0].reshape(1, d), o_w_out[0].astype(BF16), tm=512, rows_per_batch=n_lat)
    return x2.reshape(bsz, n_lat, d)
```
